```python
import jax, jax.numpy as jnp
from jax import lax
import numpy as np

D_MODEL = 1024
BATCH = 16
SEQ = 2048
DEPTH = 2

FOX_HEADS = 8
FOX_HEAD_DIM = 64
FOX_WIDTH = FOX_HEADS * FOX_HEAD_DIM
FOX_Q_BLOCK = 128
MOBA_HEADS = 8
MOBA_HEAD_DIM = 64
MOBA_WIDTH = MOBA_HEADS * MOBA_HEAD_DIM
MOBA_BLOCK = 256
MOBA_TOPK = 3
MOBA_Q_CHUNK = 16
MLSTM_HEADS = 4
MLSTM_QK_DIM = 64
MLSTM_V_DIM = 128
MLSTM_QK_WIDTH = MLSTM_HEADS * MLSTM_QK_DIM
MLSTM_V_WIDTH = MLSTM_HEADS * MLSTM_V_DIM
MLSTM_CHUNK = 64
MLSTM_CONV = 4
N_BRANCHES = 3
BRANCH_WIDTH = 512
RMS_EPS = 1e-6
NEG = -1e30

SPLIT_SIZES = (
    FOX_WIDTH, FOX_WIDTH, FOX_WIDTH, FOX_WIDTH, FOX_HEADS,
    MOBA_WIDTH, MOBA_WIDTH, MOBA_WIDTH, MOBA_WIDTH,
    2 * MLSTM_QK_WIDTH, MLSTM_V_WIDTH, MLSTM_V_WIDTH, MLSTM_V_WIDTH,
    MLSTM_HEADS, MLSTM_HEADS,
    N_BRANCHES * D_MODEL,
)
IN_WIDTH = int(sum(SPLIT_SIZES))
SPLIT_POINTS = tuple(int(v) for v in np.cumsum(SPLIT_SIZES)[:-1])

kernel_name = "hybrid_fox_moba_mlstm_gated_parallel"


def rms_norm(x, g):
    xf = x.astype(jnp.float32)
    y = xf * lax.rsqrt(jnp.mean(xf * xf, axis=-1, keepdims=True) + RMS_EPS)
    return (y * g.astype(jnp.float32)).astype(x.dtype)


def head_rms_norm(y, g, n_heads):
    b, s, w = y.shape
    yf = y.astype(jnp.float32).reshape(b, s, n_heads, w // n_heads)
    yf = yf * lax.rsqrt(jnp.mean(yf * yf, axis=-1, keepdims=True) + RMS_EPS)
    return (yf.reshape(b, s, w) * g.astype(jnp.float32)).astype(y.dtype)


def to_heads(t, n_heads):
    b, s, w = t.shape
    return t.reshape(b, s, n_heads, w // n_heads).transpose(0, 2, 1, 3)


def from_heads(t):
    b, h, s, d = t.shape
    return t.transpose(0, 2, 1, 3).reshape(b, s, h * d)


def alibi_slopes(n_heads):
    return 2.0 ** (-8.0 * (jnp.arange(n_heads, dtype=jnp.float32) + 1.0) / n_heads)


def causal_depthwise_conv(x, w):
    kw = w.shape[0]
    s = x.shape[1]
    xp = jnp.pad(x, ((0, 0), (kw - 1, 0), (0, 0)))
    return sum(xp[:, j:j + s] * w[j] for j in range(kw))


def forgetting_attention(q, k, v, log_f):
    b, h, s, d = q.shape
    nqb = s // FOX_Q_BLOCK
    scale = d ** -0.5
    c = jnp.cumsum(log_f, axis=-1)
    k_pos = jnp.arange(s)
    qb = q.reshape(b, h, nqb, FOX_Q_BLOCK, d).transpose(2, 0, 1, 3, 4)
    cb = c.reshape(b, h, nqb, FOX_Q_BLOCK).transpose(2, 0, 1, 3)
    q_pos = k_pos.reshape(nqb, FOX_Q_BLOCK)

    def one_block(args):
        q_blk, c_blk, pos = args
        logits = jnp.einsum('bhqd,bhkd->bhqk', q_blk, k).astype(jnp.float32) * scale
        logits = logits + c_blk[..., :, None] - c[..., None, :]
        logits = jnp.where(pos[:, None] >= k_pos[None, :], logits, NEG)
        p = jax.nn.softmax(logits, axis=-1)
        return jnp.einsum('bhqk,bhkd->bhqd', p.astype(v.dtype), v)

    out = lax.map(one_block, (qb, cb, q_pos))
    return out.transpose(1, 2, 0, 3, 4).reshape(b, h, s, d)


def moba_attention(q, k, v, slopes):
    b, h, s, d = q.shape
    scale = d ** -0.5
    nb = -(-s // MOBA_BLOCK)
    pad = nb * MOBA_BLOCK - s
    k_p = jnp.pad(k, ((0, 0), (0, 0), (0, pad), (0, 0)))
    v_p = jnp.pad(v, ((0, 0), (0, 0), (0, pad), (0, 0)))
    k_blocks = k_p.reshape(b, h, nb, MOBA_BLOCK, d)
    v_blocks = v_p.reshape(b, h, nb, MOBA_BLOCK, d)
    k_mean = jnp.mean(k_blocks.astype(jnp.float32), axis=3).astype(k.dtype)
    topk = min(MOBA_TOPK, nb)
    n_chunks = s // MOBA_Q_CHUNK
    qc = q.reshape(b, h, n_chunks, MOBA_Q_CHUNK, d).transpose(2, 0, 1, 3, 4)
    blk_ids = jnp.arange(nb)
    offs = jnp.arange(MOBA_BLOCK)
    bi = jnp.arange(b)[:, None, None, None]
    hi = jnp.arange(h)[None, :, None, None]
    m = slopes.astype(jnp.float32)

    def one_chunk(args):
        q_c, c_idx = args
        start = c_idx * MOBA_Q_CHUNK
        q_pos = start + jnp.arange(MOBA_Q_CHUNK)
        own = start // MOBA_BLOCK
        gate = jnp.einsum('bhqd,bhnd->bhqn', q_c, k_mean).astype(jnp.float32)
        gate = jnp.where(blk_ids < own, gate, NEG)
        _, sel = lax.top_k(gate, topk)
        sel_valid = jnp.arange(topk) < own
        k_sel = k_blocks[bi, hi, sel]
        v_sel = v_blocks[bi, hi, sel]
        sel_pos = sel[..., None] * MOBA_BLOCK + offs
        s_sel = jnp.einsum('bhqd,bhqnkd->bhqnk', q_c, k_sel).astype(jnp.float32) * scale
        dist_sel = (q_pos[:, None, None] - sel_pos).astype(jnp.float32)
        s_sel = s_sel - m[None, :, None, None, None] * dist_sel
        s_sel = jnp.where(sel_valid[:, None], s_sel, NEG)
        k_own = lax.dynamic_slice_in_dim(k_p, own * MOBA_BLOCK, MOBA_BLOCK, axis=2)
        v_own = lax.dynamic_slice_in_dim(v_p, own * MOBA_BLOCK, MOBA_BLOCK, axis=2)
        own_pos = own * MOBA_BLOCK + offs
        s_own = jnp.einsum('bhqd,bhkd->bhqk', q_c, k_own).astype(jnp.float32) * scale
        dist_own = (q_pos[:, None] - own_pos[None, :]).astype(jnp.float32)
        s_own = s_own - m[None, :, None, None] * dist_own
        s_own = jnp.where(dist_own >= 0, s_own, NEG)
        nsel = topk * MOBA_BLOCK
        logits = jnp.concatenate(
            [s_sel.reshape(b, h, MOBA_Q_CHUNK, nsel), s_own], axis=-1)
        p = jax.nn.softmax(logits, axis=-1).astype(v.dtype)
        p_sel = p[..., :nsel].reshape(b, h, MOBA_Q_CHUNK, topk, MOBA_BLOCK)
        p_own = p[..., nsel:]
        return (jnp.einsum('bhqnk,bhqnkd->bhqd', p_sel, v_sel)
                + jnp.einsum('bhqk,bhkd->bhqd', p_own, v_own))

    out = lax.map(one_chunk, (qc, jnp.arange(n_chunks)))
    return out.transpose(1, 2, 0, 3, 4).reshape(b, h, s, d)


def mlstm_chunkwise(q, k, v, log_i, log_f):
    b, h, s, dqk = q.shape
    dv = v.shape[-1]
    L = MLSTM_CHUNK
    nc = s // L
    k = k * (dqk ** -0.5)

    def chunks(t):
        return jnp.moveaxis(t.reshape(b, h, nc, L, *t.shape[3:]), 2, 0)

    xs = tuple(chunks(t) for t in (q, k, v, log_i, log_f))
    tri = jnp.tril(jnp.ones((L, L), dtype=bool))

    def step(carry, inp):
        C, n, m = carry
        q_t, k_t, v_t, li, lf = inp
        qf = q_t.astype(jnp.float32)
        kf = k_t.astype(jnp.float32)
        vf = v_t.astype(jnp.float32)
        bcum = jnp.cumsum(lf, axis=-1)
        d_intra = jnp.where(tri, bcum[..., :, None] - bcum[..., None, :] + li[..., None, :], NEG)
        d_inter = bcum + m[..., None]
        m_t = jnp.maximum(d_inter, jnp.max(d_intra, axis=-1))
        w_intra = jnp.exp(d_intra - m_t[..., None])
        w_inter = jnp.exp(d_inter - m_t)
        qk = jnp.einsum('bhtd,bhsd->bhts', qf, kf) * w_intra
        num = (jnp.einsum('bhts,bhsv->bhtv', qk, vf)
               + w_inter[..., None] * jnp.einsum('bhvd,bhtd->bhtv', C, qf))
        den = jnp.sum(qk, axis=-1) + w_inter * jnp.einsum('bhd,bhtd->bht', n, qf)
        h_t = num / jnp.maximum(jnp.abs(den), jnp.exp(-m_t))[..., None]
        b_last = bcum[..., -1]
        d_state = b_last[..., None] - bcum + li
        m_new = jnp.maximum(b_last + m, jnp.max(d_state, axis=-1))
        w_prev = jnp.exp(b_last + m - m_new)
        w_s = jnp.exp(d_state - m_new[..., None])
        C_new = w_prev[..., None, None] * C + jnp.einsum('bhs,bhsv,bhsd->bhvd', w_s, vf, kf)
        n_new = w_prev[..., None] * n + jnp.einsum('bhs,bhsd->bhd', w_s, kf)
        return (C_new, n_new, m_new), h_t

    init = (jnp.zeros((b, h, dv, dqk), jnp.float32),
            jnp.zeros((b, h, dqk), jnp.float32),
            jnp.zeros((b, h), jnp.float32))
    _, hs = lax.scan(step, init, xs)
    return jnp.moveaxis(hs, 0, 2).reshape(b, h, s, dv).astype(v.dtype)


def hybrid_layer(x, norm_g, w_in, fox_b_f, mlstm_conv_w, mlstm_b_i, mlstm_b_f,
                 mlstm_head_g, w_branch, w_out, slopes):
    b, s, _ = x.shape
    hn = rms_norm(x, norm_g)
    proj = jnp.einsum('bsd,dc->bsc', hn, w_in)
    (a_q, a_k, a_v, a_z, a_f, b_q, b_k, b_v, b_z,
     c_qk, c_v, c_o, c_z, c_i, c_f, gates) = jnp.split(proj, SPLIT_POINTS, axis=-1)

    log_f_a = jax.nn.log_sigmoid((a_f + fox_b_f).astype(jnp.float32)).transpose(0, 2, 1)
    y_a = forgetting_attention(to_heads(a_q, FOX_HEADS), to_heads(a_k, FOX_HEADS),
                               to_heads(a_v, FOX_HEADS), log_f_a)
    y_a = from_heads(y_a) * jax.nn.silu(a_z)

    y_b = moba_attention(to_heads(b_q, MOBA_HEADS), to_heads(b_k, MOBA_HEADS),
                         to_heads(b_v, MOBA_HEADS), slopes)
    y_b = from_heads(y_b) * jax.nn.silu(b_z)

    qk = jax.nn.silu(causal_depthwise_conv(c_qk, mlstm_conv_w))
    c_q, c_k = jnp.split(qk, 2, axis=-1)
    log_i = (c_i + mlstm_b_i).astype(jnp.float32).transpose(0, 2, 1)
    log_f_c = jax.nn.log_sigmoid((c_f + mlstm_b_f).astype(jnp.float32)).transpose(0, 2, 1)
    h_c = mlstm_chunkwise(to_heads(c_q, MLSTM_HEADS), to_heads(c_k, MLSTM_HEADS),
                          to_heads(c_v, MLSTM_HEADS), log_i, log_f_c)
    h_c = from_heads(h_c) * jax.nn.sigmoid(c_o)
    y_c = head_rms_norm(h_c, mlstm_head_g, MLSTM_HEADS) * jax.nn.silu(c_z)

    ys = jnp.stack([y_a, y_b, y_c], axis=2)
    branch_out = jnp.einsum('bsnw,nwd->bsnd', ys, w_branch)
    g = jax.nn.sigmoid(gates).reshape(b, s, N_BRANCHES, D_MODEL)
    merged = jnp.sum(g * branch_out, axis=2)
    return x + jnp.einsum('bsd,de->bse', merged, w_out)


def setup_inputs(seed: int = 0) -> dict:
    key = jax.random.key(seed)
    ks = jax.random.split(key, 12)
    f32 = jnp.float32
    x = jax.random.normal(ks[0], (BATCH, SEQ, D_MODEL), f32)
    norm_g = 1.0 + 0.02 * jax.random.normal(ks[1], (DEPTH, D_MODEL), f32)
    w_in = jax.random.normal(ks[2], (DEPTH, D_MODEL, IN_WIDTH), f32) * D_MODEL ** -0.5
    fox_b_f = (jnp.linspace(1.0, 4.0, FOX_HEADS, dtype=f32)[None, :]
               + 0.05 * jax.random.normal(ks[3], (DEPTH, FOX_HEADS), f32))
    mlstm_conv_w = jax.random.normal(ks[4], (DEPTH, MLSTM_CONV, 2 * MLSTM_QK_WIDTH), f32) * MLSTM_CONV ** -0.5
    mlstm_b_i = 0.1 * jax.random.normal(ks[5], (DEPTH, MLSTM_HEADS), f32)
    mlstm_b_f = (jnp.linspace(3.0, 6.0, MLSTM_HEADS, dtype=f32)[None, :]
                 + 0.05 * jax.random.normal(ks[6], (DEPTH, MLSTM_HEADS), f32))
    mlstm_head_g = 1.0 + 0.02 * jax.random.normal(ks[7], (DEPTH, MLSTM_V_WIDTH), f32)
    w_branch = jax.random.normal(ks[8], (DEPTH, N_BRANCHES, BRANCH_WIDTH, D_MODEL), f32) * BRANCH_WIDTH ** -0.5
    w_out = jax.random.normal(ks[9], (DEPTH, D_MODEL, D_MODEL), f32) * (0.5 * D_MODEL ** -0.5)
    final_norm_g = 1.0 + 0.02 * jax.random.normal(ks[10], (D_MODEL,), f32)
    return {"x": x, "norm_g": norm_g, "w_in": w_in, "fox_b_f": fox_b_f,
            "mlstm_conv_w": mlstm_conv_w, "mlstm_b_i": mlstm_b_i, "mlstm_b_f": mlstm_b_f,
            "mlstm_head_g": mlstm_head_g, "w_branch": w_branch, "w_out": w_out,
            "final_norm_g": final_norm_g}


def reference(x, norm_g, w_in, fox_b_f, mlstm_conv_w, mlstm_b_i, mlstm_b_f,
              mlstm_head_g, w_branch, w_out, final_norm_g):
    slopes = alibi_slopes(MOBA_HEADS)
    for layer in range(DEPTH):
        x = hybrid_layer(x, norm_g[layer], w_in[layer], fox_b_f[layer], mlstm_conv_w[layer],
                         mlstm_b_i[layer], mlstm_b_f[layer], mlstm_head_g[layer],
                         w_branch[layer], w_out[layer], slopes)
    return rms_norm(x, final_norm_g)
```

```python
import functools

import numpy as np
import jax
import jax.numpy as jnp
from jax import lax
from jax.experimental import pallas as pl
from jax.experimental.pallas import tpu as pltpu

F32 = jnp.float32
BF16 = jnp.bfloat16

D_MODEL = 1024
HEAD_DIM = 64
ATT_HEADS = 8
ATT_WIDTH = ATT_HEADS * HEAD_DIM
MOBA_BLOCK = 256
MOBA_TOPK = 3
ML_HEADS = 4
ML_V_DIM = 128
ML_CONV = 4
N_BRANCHES = 3
RMS_EPS = 1e-6
NEG = -1e30
ATT_SCALE = HEAD_DIM ** -0.5

LANES = 128
SUBLANES = 8
VMEM_LIMIT_BYTES = 52 * 1024 * 1024

COL_A = 0
COL_B = 4
COL_C = 8
GATE_COL0 = 6144
MAIN_WIDTH = GATE_COL0 + N_BRANCHES * D_MODEL
LANE_FOX_F = 0
LANE_ML_I = 8
LANE_ML_F = 12
GATE_ROWS = 16

ALIBI_SLOPES = tuple(
    float(v) for v in 2.0 ** (-8.0 * (np.arange(ATT_HEADS, dtype=np.float32) + 1.0) / ATT_HEADS))


def _dot(a, b):
    return jnp.dot(a, b, preferred_element_type=F32)


def _dot_nt(a, b):
    return lax.dot_general(a, b, (((1,), (1,)), ((), ())), preferred_element_type=F32)


def _sigmoid(x):
    return 1.0 / (1.0 + jnp.exp(-x))


def _compiler_params(n_axes):
    return pltpu.CompilerParams(dimension_semantics=("arbitrary",) * n_axes,
                                vmem_limit_bytes=VMEM_LIMIT_BYTES)


def _inproj_kernel(x_ref, g_ref, wm_ref, wsh_ref, wsl_ref, p_ref, gs_ref, hn_ref):
    @pl.when(pl.program_id(1) == 0)
    def _():
        x = x_ref[...]
        y = x * lax.rsqrt(jnp.mean(x * x, axis=-1, keepdims=True) + RMS_EPS) * g_ref[...]
        hi = y.astype(BF16)
        lo = (y - hi.astype(F32)).astype(BF16)
        hn_ref[...] = hi
        gs_ref[...] = _dot(hi, wsh_ref[...]) + _dot(hi, wsl_ref[...]) + _dot(lo, wsh_ref[...])

    p_ref[...] = _dot(hn_ref[...], wm_ref[...]).astype(BF16)


def _inproj(x2, g, wm, wsh, wsl, *, tm=1024, tn=1536):
    t, d = x2.shape
    n = wm.shape[1]
    return pl.pallas_call(
        _inproj_kernel,
        grid=(t // tm, n // tn),
        in_specs=[pl.BlockSpec((tm, d), lambda i, j: (i, 0)),
                  pl.BlockSpec((1, d), lambda i, j: (0, 0)),
                  pl.BlockSpec((d, tn), lambda i, j: (0, j)),
                  pl.BlockSpec((d, LANES), lambda i, j: (0, 0)),
                  pl.BlockSpec((d, LANES), lambda i, j: (0, 0))],
        out_specs=[pl.BlockSpec((tm, tn), lambda i, j: (i, j)),
                   pl.BlockSpec((tm, LANES), lambda i, j: (i, 0))],
        out_shape=[jax.ShapeDtypeStruct((t, n), BF16),
                   jax.ShapeDtypeStruct((t, LANES), F32)],
        scratch_shapes=[pltpu.VMEM((tm, d), BF16)],
        compiler_params=_compiler_params(2),
        name="inproj",
    )(x2, g, wm, wsh, wsl)


def _gates_kernel(gs_ref, bias_ref, cs_ref, cst_ref, rawt_ref, *, cb):
    s = gs_ref.shape[0]
    lane = lax.broadcasted_iota(jnp.int32, (1, LANES), 1)
    is_log_i = (lane >= LANE_ML_I) & (lane < LANE_ML_F)
    r = lax.broadcasted_iota(jnp.int32, (cb, cb), 0)
    c = lax.broadcasted_iota(jnp.int32, (cb, cb), 1)
    tri = jnp.where(r >= c, 1.0, 0.0).astype(BF16)
    carry = jnp.zeros((1, LANES), F32)
    for blk in range(s // cb):
        rows = slice(blk * cb, (blk + 1) * cb)
        g = gs_ref[rows, :] + bias_ref[...]
        log_sig = jnp.minimum(g, 0.0) - jnp.log(1.0 + jnp.exp(-jnp.abs(g)))
        raw = jnp.where(is_log_i, g, log_sig)
        hi = raw.astype(BF16)
        r1 = raw - hi.astype(F32)
        mid = r1.astype(BF16)
        lo = (r1 - mid.astype(F32)).astype(BF16)
        cs = _dot(tri, hi) + _dot(tri, mid) + _dot(tri, lo) + carry
        carry = cs[cb - 1:cb, :]
        cs_ref[rows, :] = cs
        cst_ref[:, rows] = cs.T[:GATE_ROWS, :]
        rawt_ref[:, rows] = raw.T[:GATE_ROWS, :]


def _gates(gs3, bias, *, cb=256):
    b, s, _ = gs3.shape
    col = pl.BlockSpec((None, s, LANES), lambda i: (i, 0, 0))
    row = pl.BlockSpec((None, GATE_ROWS, s), lambda i: (i, 0, 0))
    return pl.pallas_call(
        functools.partial(_gates_kernel, cb=cb),
        grid=(b,),
        in_specs=[col, pl.BlockSpec((1, LANES), lambda i: (0, 0))],
        out_specs=[col, row, row],
        out_shape=[jax.ShapeDtypeStruct((b, s, LANES), F32),
                   jax.ShapeDtypeStruct((b, GATE_ROWS, s), F32),
                   jax.ShapeDtypeStruct((b, GATE_ROWS, s), F32)],
        compiler_params=_compiler_params(1),
        name="gates",
    )(gs3, bias)


def _softmax_first(s, v):
    m = jnp.max(s, axis=1, keepdims=True)
    p = jnp.exp(s - m)
    l = jnp.sum(p, axis=1, keepdims=True)
    return m, l, _dot(p.astype(BF16), v)


def _softmax_step(carry, s, v):
    m, l, acc = carry
    m_new = jnp.maximum(m, jnp.max(s, axis=1, keepdims=True))
    alpha = jnp.exp(m - m_new)
    p = jnp.exp(s - m_new)
    l = alpha * l + jnp.sum(p, axis=1, keepdims=True)
    acc = alpha * acc + _dot(p.astype(BF16), v)
    return m_new, l, acc


def _head_lane_mask(hh):
    lane = lax.broadcasted_iota(jnp.int32, (1, LANES), 1)
    return (lane >= HEAD_DIM * hh) & (lane < HEAD_DIM * (hh + 1))


def _fox_kernel(q_ref, k_ref, v_ref, z_ref, cc_ref, cr_ref, o_ref, *, tq):
    i = pl.program_id(1)
    row = lax.broadcasted_iota(jnp.int32, (tq, tq), 0)
    col = lax.broadcasted_iota(jnp.int32, (tq, tq), 1)
    causal = row >= col
    lane = lax.broadcasted_iota(jnp.int32, (1, LANES), 1)
    diag0 = pl.multiple_of(i * tq, tq)
    for hp in range(ATT_HEADS // 2):
        lanes = slice(hp * LANES, (hp + 1) * LANES)
        qp = q_ref[:, lanes] * jnp.asarray(ATT_SCALE, BF16)
        outs = []
        for hh in range(2):
            h = 2 * hp + hh
            qh = jnp.where(_head_lane_mask(hh), qp, jnp.zeros_like(qp))
            c_t = cc_ref[:, LANE_FOX_F + h:LANE_FOX_F + h + 1]

            def logits(start):
                kj = k_ref[pl.ds(start, tq), lanes]
                c_s = cr_ref[LANE_FOX_F + h:LANE_FOX_F + h + 1, pl.ds(start, tq)]
                return _dot_nt(qh, kj) + (c_t - c_s)

            s0 = jnp.where(causal, logits(diag0), NEG)
            carry = _softmax_first(s0, v_ref[pl.ds(diag0, tq), lanes])

            def body(j, carry):
                start = pl.multiple_of(j * tq, tq)
                return _softmax_step(carry, logits(start), v_ref[pl.ds(start, tq), lanes])

            _, l, acc = lax.fori_loop(0, i, body, carry)
            outs.append(acc / l)
        pair = jnp.where(lane < HEAD_DIM, outs[0], outs[1])
        z = z_ref[:, lanes].astype(F32)
        o_ref[:, lanes] = (pair * (z * _sigmoid(z))).astype(BF16)


def _fox(p3, cs, cst, *, tq=256):
    b, s, _ = p3.shape
    tile = lambda c: pl.BlockSpec((None, tq, ATT_WIDTH), lambda bi, i, c=c: (bi, i, c))
    full = lambda c: pl.BlockSpec((None, s, ATT_WIDTH), lambda bi, i, c=c: (bi, 0, c))
    return pl.pallas_call(
        functools.partial(_fox_kernel, tq=tq),
        grid=(b, s // tq),
        in_specs=[tile(COL_A + 0), full(COL_A + 1), full(COL_A + 2), tile(COL_A + 3),
                  pl.BlockSpec((None, tq, LANES), lambda bi, i: (bi, i, 0)),
                  pl.BlockSpec((None, GATE_ROWS, s), lambda bi, i: (bi, 0, 0))],
        out_specs=pl.BlockSpec((None, tq, ATT_WIDTH), lambda bi, i: (bi, i, 0)),
        out_shape=jax.ShapeDtypeStruct((b, s, ATT_WIDTH), BF16),
        compiler_params=_compiler_params(2),
        name="fox",
    )(p3, p3, p3, p3, cs, cst)


def _moba_kernel(q_ref, k_ref, v_ref, z_ref, o_ref, kmt_ref, md_ref):
    i = pl.program_id(1)
    bs = MOBA_BLOCK
    nb = k_ref.shape[0] // bs
    row = lax.broadcasted_iota(jnp.int32, (bs, bs), 0)
    col = lax.broadcasted_iota(jnp.int32, (bs, bs), 1)

    @pl.when(i == 0)
    def _():
        r8 = lax.broadcasted_iota(jnp.int32, (ATT_HEADS, ATT_WIDTH), 0)
        head_of_lane = lax.broadcasted_iota(jnp.int32, (ATT_HEADS, ATT_WIDTH), 1) // HEAD_DIM
        kmt_ref[...] = jnp.zeros_like(kmt_ref)
        for j in range(nb):
            km = jnp.mean(k_ref[j * bs:(j + 1) * bs, :].astype(F32), axis=0, keepdims=True)
            kmt_ref[j * ATT_HEADS:(j + 1) * ATT_HEADS, :] = jnp.where(
                r8 == head_of_lane, jnp.broadcast_to(km, (ATT_HEADS, ATT_WIDTH)), 0.0)
        dist = (row - col).astype(F32)
        for h in range(ATT_HEADS):
            md_ref[h] = dist * (-ALIBI_SLOPES[h])

    gate = _dot_nt(q_ref[...], kmt_ref[...].astype(BF16))
    lane = lax.broadcasted_iota(jnp.int32, (bs, LANES), 1)
    blk_of_lane = lane >> 3
    valid = blk_of_lane < i
    gate = jnp.where(valid, gate, NEG)
    rank = jnp.zeros((bs, LANES), F32)
    n_slots = LANES // ATT_HEADS
    for r in range(1, n_slots):
        other = pltpu.roll(gate, ATT_HEADS * r, axis=1)
        other_is_lower = jnp.where(blk_of_lane >= r, 1.0, 0.0)
        rank = rank + jnp.where(other > gate, 1.0, 0.0) + jnp.where(other == gate, other_is_lower, 0.0)
    slope_of_lane = jnp.zeros((bs, LANES), F32)
    for h in range(ATT_HEADS):
        slope_of_lane = jnp.where((lane & (ATT_HEADS - 1)) == h, ALIBI_SLOPES[h], slope_of_lane)
    blocks_back = (i - blk_of_lane).astype(F32)
    bias = jnp.where(valid & (rank < MOBA_TOPK), -(slope_of_lane * bs) * blocks_back, NEG)

    causal = row >= col
    lane1 = lax.broadcasted_iota(jnp.int32, (1, LANES), 1)
    own0 = pl.multiple_of(i * bs, bs)
    for hp in range(ATT_HEADS // 2):
        lanes = slice(hp * LANES, (hp + 1) * LANES)
        qp = q_ref[:, lanes] * jnp.asarray(ATT_SCALE, BF16)
        outs = []
        for hh in range(2):
            h = 2 * hp + hh
            qh = jnp.where(_head_lane_mask(hh), qp, jnp.zeros_like(qp))

            s0 = _dot_nt(qh, k_ref[pl.ds(own0, bs), lanes]) + md_ref[h]
            carry = _softmax_first(jnp.where(causal, s0, NEG), v_ref[pl.ds(own0, bs), lanes])

            def body(j, carry):
                start = pl.multiple_of(j * bs, bs)
                colbias = jnp.sum(jnp.where(lane == j * ATT_HEADS + h, bias, 0.0),
                                  axis=1, keepdims=True)
                s = _dot_nt(qh, k_ref[pl.ds(start, bs), lanes]) + md_ref[h] + colbias
                return _softmax_step(carry, s, v_ref[pl.ds(start, bs), lanes])

            _, l, acc = lax.fori_loop(0, i, body, carry)
            outs.append(acc / l)
        pair = jnp.where(lane1 < HEAD_DIM, outs[0], outs[1])
        z = z_ref[:, lanes].astype(F32)
        o_ref[:, lanes] = (pair * (z * _sigmoid(z))).astype(BF16)


def _moba(p3):
    b, s, _ = p3.shape
    bs = MOBA_BLOCK
    assert s % bs == 0 and (s // bs) * ATT_HEADS <= LANES
    tile = lambda c: pl.BlockSpec((None, bs, ATT_WIDTH), lambda bi, i, c=c: (bi, i, c))
    full = lambda c: pl.BlockSpec((None, s, ATT_WIDTH), lambda bi, i, c=c: (bi, 0, c))
    return pl.pallas_call(
        _moba_kernel,
        grid=(b, s // bs),
        in_specs=[tile(COL_B + 0), full(COL_B + 1), full(COL_B + 2), tile(COL_B + 3)],
        out_specs=pl.BlockSpec((None, bs, ATT_WIDTH), lambda bi, i: (bi, i, 0)),
        out_shape=jax.ShapeDtypeStruct((b, s, ATT_WIDTH), BF16),
        scratch_shapes=[pltpu.VMEM((LANES, ATT_WIDTH), F32),
                        pltpu.VMEM((ATT_HEADS, bs, bs), F32)],
        compiler_params=_compiler_params(2),
        name="moba",
    )(p3, p3, p3, p3)


def _mlstm_kernel(qk_ref, v_ref, og_ref, z_ref, w_ref, cc_ref, cr_ref, rr_ref, hg_ref, y_ref,
                  cbuf_ref, cst_ref, mst_ref, fprev_ref, *, L):
    c = pl.program_id(1)
    halo = SUBLANES

    @pl.when(c == 0)
    def _():
        cbuf_ref[0:halo, :] = jnp.zeros((halo, cbuf_ref.shape[1]), F32)
        cst_ref[...] = jnp.zeros_like(cst_ref)
        mst_ref[...] = jnp.zeros_like(mst_ref)
        fprev_ref[...] = jnp.zeros_like(fprev_ref)

    @pl.when(c > 0)
    def _():
        cbuf_ref[0:halo, :] = cbuf_ref[L:L + halo, :]

    cbuf_ref[halo:halo + L, :] = qk_ref[...].astype(F32)
    conv = None
    for j in range(ML_CONV):
        off = halo - (ML_CONV - 1) + j
        term = cbuf_ref[off:off + L, :] * w_ref[j:j + 1, :]
        conv = term if conv is None else conv + term
    qk = conv * _sigmoid(conv)
    qk_w = ML_HEADS * HEAD_DIM
    q_b = qk[:, :qk_w].astype(BF16)
    k_s = qk[:, qk_w:] * ATT_SCALE
    k_b = k_s.astype(BF16)
    k_t = k_s.T

    row = lax.broadcasted_iota(jnp.int32, (L, L), 0)
    col = lax.broadcasted_iota(jnp.int32, (L, L), 1)
    tri = row >= col
    lane1 = lax.broadcasted_iota(jnp.int32, (1, LANES), 1)
    ones_col = jnp.broadcast_to(jnp.where(lane1 == 0, 1.0, 0.0), (L, LANES)).astype(BF16)

    for h in range(ML_HEADS):
        hp, hh = divmod(h, 2)
        lanes = slice(hp * LANES, (hp + 1) * LANES)
        qh = jnp.where(_head_lane_mask(hh), q_b[:, lanes], jnp.zeros((L, LANES), BF16))
        f_c = cc_ref[:, LANE_ML_F + h:LANE_ML_F + h + 1]
        g_r = cr_ref[LANE_ML_F + h:LANE_ML_F + h + 1, :] - rr_ref[LANE_ML_I + h:LANE_ML_I + h + 1, :]
        f0 = fprev_ref[0:1, LANE_ML_F + h:LANE_ML_F + h + 1]
        m_prev = mst_ref[h, 0:1, 0:1]

        d_intra = jnp.where(tri, f_c - g_r, NEG)
        d_inter = f_c - f0 + m_prev
        m_t = jnp.maximum(d_inter, jnp.max(d_intra, axis=1, keepdims=True))
        w_intra = jnp.exp(d_intra - m_t)
        w_inter = jnp.exp(d_inter - m_t)
        s_mat = (_dot_nt(qh, k_b[:, lanes]) * w_intra).astype(BF16)
        v_aug = jnp.concatenate([v_ref[:, h * ML_V_DIM:(h + 1) * ML_V_DIM], ones_col], axis=1)
        c_prev = cst_ref[hp]
        num = _dot(s_mat, v_aug) + w_inter * _dot(qh, c_prev.astype(BF16))
        den = num[:, ML_V_DIM:ML_V_DIM + 1]
        hv = num[:, :ML_V_DIM] / jnp.maximum(jnp.abs(den), jnp.exp(-m_t))

        f_last = f_c[L - 1:L, :]
        d_state = f_last - g_r
        m_new = jnp.maximum(f_last - f0 + m_prev, jnp.max(d_state, axis=1, keepdims=True))
        w_prev = jnp.exp(f_last - f0 + m_prev - m_new)
        w_s = jnp.exp(d_state - m_new)
        rows = slice(hh * HEAD_DIM, (hh + 1) * HEAD_DIM)
        kw = (k_t[h * HEAD_DIM:(h + 1) * HEAD_DIM, :] * w_s).astype(BF16)
        cst_ref[hp, rows, :] = w_prev * c_prev[rows, :] + _dot(kw, v_aug)
        mst_ref[h] = jnp.broadcast_to(m_new, mst_ref.shape[1:])

        vl = slice(h * ML_V_DIM, (h + 1) * ML_V_DIM)
        ho = hv * _sigmoid(og_ref[:, vl].astype(F32))
        yn = ho * lax.rsqrt(jnp.mean(ho * ho, axis=1, keepdims=True) + RMS_EPS) * hg_ref[:, vl]
        z = z_ref[:, vl].astype(F32)
        y_ref[:, vl] = (yn * (z * _sigmoid(z))).astype(BF16)

    fprev_ref[...] = cc_ref[L - 1:L, :]


def _mlstm(p3, conv_w, cs, cst, rawt, head_g, *, L=256):
    b, s, _ = p3.shape
    w512 = ML_HEADS * ML_V_DIM
    tile = lambda c: pl.BlockSpec((None, L, w512), lambda bi, i, c=c: (bi, i, c))
    rowt = pl.BlockSpec((None, GATE_ROWS, L), lambda bi, i: (bi, 0, i))
    return pl.pallas_call(
        functools.partial(_mlstm_kernel, L=L),
        grid=(b, s // L),
        in_specs=[tile(COL_C + 0), tile(COL_C + 1), tile(COL_C + 2), tile(COL_C + 3),
                  pl.BlockSpec((ML_CONV, w512), lambda bi, i: (0, 0)),
                  pl.BlockSpec((None, L, LANES), lambda bi, i: (bi, i, 0)),
                  rowt, rowt,
                  pl.BlockSpec((1, w512), lambda bi, i: (0, 0))],
        out_specs=pl.BlockSpec((None, L, w512), lambda bi, i: (bi, i, 0)),
        out_shape=jax.ShapeDtypeStruct((b, s, w512), BF16),
        scratch_shapes=[pltpu.VMEM((L + SUBLANES, w512), F32),
                        pltpu.VMEM((ML_HEADS // 2, 2 * HEAD_DIM, 2 * ML_V_DIM), F32),
                        pltpu.VMEM((ML_HEADS, SUBLANES, LANES), F32),
                        pltpu.VMEM((1, LANES), F32)],
        compiler_params=_compiler_params(2),
        name="mlstm",
    )(p3, p3, p3, p3, conv_w, cs, cst, rawt, head_g)


def _merge_kernel(ya_ref, yb_ref, yc_ref, g_ref, x_ref, wb_ref, wo_ref, fg_ref, o_ref, *, final):
    d = x_ref.shape[1]
    merged = None
    for n, y_ref in enumerate((ya_ref, yb_ref, yc_ref)):
        gate = _sigmoid(g_ref[:, n * d:(n + 1) * d].astype(F32))
        term = gate * _dot(y_ref[...], wb_ref[n])
        merged = term if merged is None else merged + term
    out = x_ref[...] + _dot(merged.astype(BF16), wo_ref[...])
    if final:
        out = out * lax.rsqrt(jnp.mean(out * out, axis=-1, keepdims=True) + RMS_EPS) * fg_ref[...]
    o_ref[...] = out


def _merge(ya, yb, yc, p2, x2, wb, wo, fg, *, final, tm=512):
    t, d = x2.shape
    w = ya.shape[1]
    ytile = pl.BlockSpec((tm, w), lambda i: (i, 0))
    return pl.pallas_call(
        functools.partial(_merge_kernel, final=final),
        grid=(t // tm,),
        in_specs=[ytile, ytile, ytile,
                  pl.BlockSpec((tm, N_BRANCHES * d), lambda i: (i, GATE_COL0 // (N_BRANCHES * d))),
                  pl.BlockSpec((tm, d), lambda i: (i, 0)),
                  pl.BlockSpec((N_BRANCHES, w, d), lambda i: (0, 0, 0)),
                  pl.BlockSpec((d, d), lambda i: (0, 0)),
                  pl.BlockSpec((1, d), lambda i: (0, 0))],
        out_specs=pl.BlockSpec((tm, d), lambda i: (i, 0)),
        out_shape=jax.ShapeDtypeStruct((t, d), F32),
        compiler_params=_compiler_params(1),
        name="merge_final" if final else "merge",
    )(ya, yb, yc, p2, x2, wb, wo, fg)


def _regroup_in_weights(w):
    aw = 4 * ATT_WIDTH
    a_f0 = aw
    b0 = a_f0 + ATT_HEADS
    c0 = b0 + aw
    c_i0 = c0 + 4 * ML_HEADS * ML_V_DIM
    g0 = c_i0 + 2 * ML_HEADS
    main = jnp.concatenate([w[:, :a_f0], w[:, b0:c_i0], w[:, g0:]], axis=1)
    small = jnp.concatenate([w[:, a_f0:b0], w[:, c_i0:g0]], axis=1)
    small = jnp.pad(small, ((0, 0), (0, LANES - small.shape[1])))
    return main, small


def kernel(x, norm_g, w_in, fox_b_f, mlstm_conv_w, mlstm_b_i, mlstm_b_f, mlstm_head_g, w_branch, w_out,
           final_norm_g):
    b, s, d = x.shape
    depth = w_in.shape[0]
    x2 = x.reshape(b * s, d)
    fg = final_norm_g.reshape(1, d)
    for layer in range(depth):
        main, small = _regroup_in_weights(w_in[layer])
        assert main.shape[1] == MAIN_WIDTH
        wm = main.astype(BF16)
        wsh = small.astype(BF16)
        wsl = (small - wsh.astype(F32)).astype(BF16)
        bias = jnp.concatenate([fox_b_f[layer], mlstm_b_i[layer], mlstm_b_f[layer]])
        bias = jnp.pad(bias, (0, LANES - bias.shape[0])).reshape(1, LANES)

        p2, gs = _inproj(x2, norm_g[layer].reshape(1, d), wm, wsh, wsl)
        p3 = p2.reshape(b, s, MAIN_WIDTH)
        cs, cst, rawt = _gates(gs.reshape(b, s, LANES), bias)
        ya = _fox(p3, cs, cst)
        yb = _moba(p3)
        yc = _mlstm(p3, mlstm_conv_w[layer], cs, cst, rawt, mlstm_head_g[layer].reshape(1, -1))
        w512 = ya.shape[-1]
        x2 = _merge(ya.reshape(b * s, w512), yb.reshape(b * s, w512), yc.reshape(b * s, w512), p2, x2,
                    w_branch[layer].astype(BF16), w_out[layer].astype(BF16), fg,
                    final=(layer == depth - 1))
    return x2.reshape(b, s, d)
```

```python
import functools

import numpy as np
import jax
import jax.numpy as jnp
from jax import lax
from jax.experimental import pallas as pl
from jax.experimental.pallas import tpu as pltpu

F32 = jnp.float32
BF16 = jnp.bfloat16

D_MODEL = 1024
HEAD_DIM = 64
ATT_HEADS = 8
ATT_WIDTH = ATT_HEADS * HEAD_DIM
MOBA_BLOCK = 256
MOBA_TOPK = 3
ML_HEADS = 4
ML_V_DIM = 128
ML_CONV = 4
N_BRANCHES = 3
RMS_EPS = 1e-6
NEG = -1e30
ATT_SCALE = HEAD_DIM ** -0.5

LANES = 128
SUBLANES = 8
VMEM_LIMIT_BYTES = 52 * 1024 * 1024

COL_A = 0
COL_B = 4
COL_C = 8
GATE_COL0 = 6144
MAIN_WIDTH = GATE_COL0 + N_BRANCHES * D_MODEL
LANE_FOX_F = 0
LANE_ML_I = 8
LANE_ML_F = 12
GATE_ROWS = 16

ALIBI_SLOPES = tuple(
    float(v) for v in 2.0 ** (-8.0 * (np.arange(ATT_HEADS, dtype=np.float32) + 1.0) / ATT_HEADS))


def _dot(a, b):
    return jnp.dot(a, b, preferred_element_type=F32)


def _dot_nt(a, b):
    return lax.dot_general(a, b, (((1,), (1,)), ((), ())), preferred_element_type=F32)


def _sigmoid(x):
    return 1.0 / (1.0 + jnp.exp(-x))


def _compiler_params(n_axes):
    return pltpu.CompilerParams(dimension_semantics=("arbitrary",) * n_axes,
                                vmem_limit_bytes=VMEM_LIMIT_BYTES)


def _inproj_kernel(x_ref, g_ref, wm_ref, wsh_ref, wsl_ref, p_ref, gs_ref, hn_ref):
    @pl.when(pl.program_id(1) == 0)
    def _():
        x = x_ref[...]
        y = x * lax.rsqrt(jnp.mean(x * x, axis=-1, keepdims=True) + RMS_EPS) * g_ref[...]
        hi = y.astype(BF16)
        lo = (y - hi.astype(F32)).astype(BF16)
        hn_ref[...] = hi
        gs_ref[...] = _dot(hi, wsh_ref[...]) + _dot(hi, wsl_ref[...]) + _dot(lo, wsh_ref[...])

    p_ref[...] = _dot(hn_ref[...], wm_ref[...]).astype(BF16)


def _inproj(x2, g, wm, wsh, wsl, *, tm=1024, tn=1536):
    t, d = x2.shape
    n = wm.shape[1]
    return pl.pallas_call(
        _inproj_kernel,
        grid=(t // tm, n // tn),
        in_specs=[pl.BlockSpec((tm, d), lambda i, j: (i, 0)),
                  pl.BlockSpec((1, d), lambda i, j: (0, 0)),
                  pl.BlockSpec((d, tn), lambda i, j: (0, j)),
                  pl.BlockSpec((d, LANES), lambda i, j: (0, 0)),
                  pl.BlockSpec((d, LANES), lambda i, j: (0, 0))],
        out_specs=[pl.BlockSpec((tm, tn), lambda i, j: (i, j)),
                   pl.BlockSpec((tm, LANES), lambda i, j: (i, 0))],
        out_shape=[jax.ShapeDtypeStruct((t, n), BF16),
                   jax.ShapeDtypeStruct((t, LANES), F32)],
        scratch_shapes=[pltpu.VMEM((tm, d), BF16)],
        compiler_params=_compiler_params(2),
        name="inproj",
    )(x2, g, wm, wsh, wsl)


def _gates_kernel(gs_ref, bias_ref, cs_ref, cst_ref, rawt_ref, *, cb):
    s = gs_ref.shape[0]
    lane = lax.broadcasted_iota(jnp.int32, (1, LANES), 1)
    is_log_i = (lane >= LANE_ML_I) & (lane < LANE_ML_F)
    r = lax.broadcasted_iota(jnp.int32, (cb, cb), 0)
    c = lax.broadcasted_iota(jnp.int32, (cb, cb), 1)
    tri = jnp.where(r >= c, 1.0, 0.0).astype(BF16)
    carry = jnp.zeros((1, LANES), F32)
    for blk in range(s // cb):
        rows = slice(blk * cb, (blk + 1) * cb)
        g = gs_ref[rows, :] + bias_ref[...]
        log_sig = jnp.minimum(g, 0.0) - jnp.log(1.0 + jnp.exp(-jnp.abs(g)))
        raw = jnp.where(is_log_i, g, log_sig)
        hi = raw.astype(BF16)
        r1 = raw - hi.astype(F32)
        mid = r1.astype(BF16)
        lo = (r1 - mid.astype(F32)).astype(BF16)
        cs = _dot(tri, hi) + _dot(tri, mid) + _dot(tri, lo) + carry
        carry = cs[cb - 1:cb, :]
        cs_ref[rows, :] = cs
        cst_ref[:, rows] = cs.T[:GATE_ROWS, :]
        rawt_ref[:, rows] = raw.T[:GATE_ROWS, :]


def _gates(gs3, bias, *, cb=256):
    b, s, _ = gs3.shape
    col = pl.BlockSpec((None, s, LANES), lambda i: (i, 0, 0))
    row = pl.BlockSpec((None, GATE_ROWS, s), lambda i: (i, 0, 0))
    return pl.pallas_call(
        functools.partial(_gates_kernel, cb=cb),
        grid=(b,),
        in_specs=[col, pl.BlockSpec((1, LANES), lambda i: (0, 0))],
        out_specs=[col, row, row],
        out_shape=[jax.ShapeDtypeStruct((b, s, LANES), F32),
                   jax.ShapeDtypeStruct((b, GATE_ROWS, s), F32),
                   jax.ShapeDtypeStruct((b, GATE_ROWS, s), F32)],
        compiler_params=_compiler_params(1),
        name="gates",
    )(gs3, bias)


def _softmax_first(s, vt):
    m = jnp.max(s, axis=0, keepdims=True)
    p = jnp.exp(s - m)
    l = jnp.sum(p, axis=0, keepdims=True)
    return m, l, _dot(vt, p.astype(BF16))


def _softmax_step(carry, s, vt):
    m, l, acc = carry
    m_new = jnp.maximum(m, jnp.max(s, axis=0, keepdims=True))
    alpha = jnp.exp(m - m_new)
    p = jnp.exp(s - m_new)
    l = alpha * l + jnp.sum(p, axis=0, keepdims=True)
    acc = alpha * acc + _dot(vt, p.astype(BF16))
    return m_new, l, acc


def _head_lane_mask(hh):
    lane = lax.broadcasted_iota(jnp.int32, (1, LANES), 1)
    return (lane >= HEAD_DIM * hh) & (lane < HEAD_DIM * (hh + 1))


def _transpose_values(i, v_ref, vt_ref, blk):
    @pl.when(i == 0)
    def _():
        for b0 in range(0, v_ref.shape[0], blk):
            for c0 in range(0, v_ref.shape[1], LANES):
                vb = v_ref[b0:b0 + blk, c0:c0 + LANES].astype(F32)
                vt_ref[c0:c0 + LANES, b0:b0 + blk] = vb.T.astype(BF16)


def _flash_all_heads(i, tq, q_ref, k_ref, z_ref, o_ref, qt_ref, vt_ref, logits_fn):
    n_pairs = ATT_HEADS // 2
    chan = lax.broadcasted_iota(jnp.int32, (LANES, 1), 0)
    for hp in range(n_pairs):
        qp = (q_ref[:, hp * LANES:(hp + 1) * LANES] * jnp.asarray(ATT_SCALE, BF16)).astype(F32).T
        for hh in range(2):
            own = (chan >= HEAD_DIM * hh) & (chan < HEAD_DIM * (hh + 1))
            qt_ref[2 * hp + hh] = jnp.where(own, qp, 0.0).astype(BF16)

    def tile(start, diag, state):
        scores = []
        for hp in range(n_pairs):
            kj = k_ref[pl.ds(start, tq), hp * LANES:(hp + 1) * LANES]
            for hh in range(2):
                h = 2 * hp + hh
                scores.append(logits_fn(h, _dot(kj, qt_ref[h]), start, diag))
        new_state = []
        for h in range(ATT_HEADS):
            vt = vt_ref[h * HEAD_DIM:(h + 1) * HEAD_DIM, pl.ds(start, tq)]
            new_state.append(_softmax_first(scores[h], vt) if diag
                             else _softmax_step(state[h], scores[h], vt))
        return tuple(new_state)

    state = tile(pl.multiple_of(i * tq, tq), True, None)
    state = lax.fori_loop(0, i, lambda j, st: tile(pl.multiple_of(j * tq, tq), False, st), state)

    for hp in range(n_pairs):
        lanes = slice(hp * LANES, (hp + 1) * LANES)
        (_, l0, acc0), (_, l1, acc1) = state[2 * hp], state[2 * hp + 1]
        pair = jnp.concatenate([acc0 / l0, acc1 / l1], axis=0).T
        z = z_ref[:, lanes].astype(F32)
        o_ref[:, lanes] = (pair * (z * _sigmoid(z))).astype(BF16)


def _flash_scratch(tq, s):
    return [pltpu.VMEM((ATT_HEADS, LANES, tq), BF16),
            pltpu.VMEM((ATT_WIDTH, s), BF16)]


def _fox_kernel(q_ref, k_ref, v_ref, z_ref, cc_ref, cr_ref, o_ref, qt_ref, vt_ref, *, tq):
    i = pl.program_id(1)
    _transpose_values(i, v_ref, vt_ref, tq)
    key = lax.broadcasted_iota(jnp.int32, (tq, tq), 0)
    qry = lax.broadcasted_iota(jnp.int32, (tq, tq), 1)
    causal = key <= qry
    q0 = pl.multiple_of(i * tq, tq)

    def logits(h, qk, start, diag):
        c_q = cr_ref[LANE_FOX_F + h:LANE_FOX_F + h + 1, pl.ds(q0, tq)]
        c_k = cc_ref[pl.ds(start, tq), LANE_FOX_F + h:LANE_FOX_F + h + 1]
        s = qk + (c_q - c_k)
        return jnp.where(causal, s, NEG) if diag else s

    _flash_all_heads(i, tq, q_ref, k_ref, z_ref, o_ref, qt_ref, vt_ref, logits)


def _fox(p3, cs, cst, *, tq=256):
    b, s, _ = p3.shape
    tile = lambda c: pl.BlockSpec((None, tq, ATT_WIDTH), lambda bi, i, c=c: (bi, i, c))
    full = lambda c: pl.BlockSpec((None, s, ATT_WIDTH), lambda bi, i, c=c: (bi, 0, c))
    return pl.pallas_call(
        functools.partial(_fox_kernel, tq=tq),
        grid=(b, s // tq),
        in_specs=[tile(COL_A + 0), full(COL_A + 1), full(COL_A + 2), tile(COL_A + 3),
                  pl.BlockSpec((None, s, LANES), lambda bi, i: (bi, 0, 0)),
                  pl.BlockSpec((None, GATE_ROWS, s), lambda bi, i: (bi, 0, 0))],
        out_specs=pl.BlockSpec((None, tq, ATT_WIDTH), lambda bi, i: (bi, i, 0)),
        out_shape=jax.ShapeDtypeStruct((b, s, ATT_WIDTH), BF16),
        scratch_shapes=_flash_scratch(tq, s),
        compiler_params=_compiler_params(2),
        name="fox",
    )(p3, p3, p3, p3, cs, cst)


def _moba_kernel(q_ref, k_ref, v_ref, z_ref, o_ref, kmt_ref, md_ref, biast_ref, qt_ref, vt_ref):
    i = pl.program_id(1)
    bs = MOBA_BLOCK
    nb = k_ref.shape[0] // bs
    _transpose_values(i, v_ref, vt_ref, bs)
    key = lax.broadcasted_iota(jnp.int32, (bs, bs), 0)
    qry = lax.broadcasted_iota(jnp.int32, (bs, bs), 1)

    @pl.when(i == 0)
    def _():
        r8 = lax.broadcasted_iota(jnp.int32, (ATT_HEADS, ATT_WIDTH), 0)
        head_of_lane = lax.broadcasted_iota(jnp.int32, (ATT_HEADS, ATT_WIDTH), 1) // HEAD_DIM
        kmt_ref[...] = jnp.zeros_like(kmt_ref)
        for j in range(nb):
            km = jnp.mean(k_ref[j * bs:(j + 1) * bs, :].astype(F32), axis=0, keepdims=True)
            kmt_ref[j * ATT_HEADS:(j + 1) * ATT_HEADS, :] = jnp.where(
                r8 == head_of_lane, jnp.broadcast_to(km, (ATT_HEADS, ATT_WIDTH)), 0.0)
        dist = (qry - key).astype(F32)
        for h in range(ATT_HEADS):
            md_ref[h] = dist * (-ALIBI_SLOPES[h])

    gate = _dot_nt(q_ref[...], kmt_ref[...].astype(BF16))
    lane = lax.broadcasted_iota(jnp.int32, (bs, LANES), 1)
    blk_of_lane = lane >> 3
    valid = blk_of_lane < i
    gate = jnp.where(valid, gate, NEG)
    rank = jnp.zeros((bs, LANES), F32)
    n_slots = LANES // ATT_HEADS
    for r in range(1, n_slots):
        other = pltpu.roll(gate, ATT_HEADS * r, axis=1)
        other_is_lower = jnp.where(blk_of_lane >= r, 1.0, 0.0)
        rank = rank + jnp.where(other > gate, 1.0, 0.0) + jnp.where(other == gate, other_is_lower, 0.0)
    slope_of_lane = jnp.zeros((bs, LANES), F32)
    for h in range(ATT_HEADS):
        slope_of_lane = jnp.where((lane & (ATT_HEADS - 1)) == h, ALIBI_SLOPES[h], slope_of_lane)
    blocks_back = (i - blk_of_lane).astype(F32)
    bias = jnp.where(valid & (rank < MOBA_TOPK), -(slope_of_lane * bs) * blocks_back, NEG)
    biast_ref[...] = bias.T

    causal = key <= qry

    def logits(h, qk, start, diag):
        s = qk + md_ref[h]
        if diag:
            return jnp.where(causal, s, NEG)
        return s + biast_ref[pl.ds((start // bs) * ATT_HEADS + h, 1), :]

    _flash_all_heads(i, bs, q_ref, k_ref, z_ref, o_ref, qt_ref, vt_ref, logits)


def _moba(p3):
    b, s, _ = p3.shape
    bs = MOBA_BLOCK
    assert s % bs == 0 and (s // bs) * ATT_HEADS <= LANES
    tile = lambda c: pl.BlockSpec((None, bs, ATT_WIDTH), lambda bi, i, c=c: (bi, i, c))
    full = lambda c: pl.BlockSpec((None, s, ATT_WIDTH), lambda bi, i, c=c: (bi, 0, c))
    return pl.pallas_call(
        _moba_kernel,
        grid=(b, s // bs),
        in_specs=[tile(COL_B + 0), full(COL_B + 1), full(COL_B + 2), tile(COL_B + 3)],
        out_specs=pl.BlockSpec((None, bs, ATT_WIDTH), lambda bi, i: (bi, i, 0)),
        out_shape=jax.ShapeDtypeStruct((b, s, ATT_WIDTH), BF16),
        scratch_shapes=[pltpu.VMEM((LANES, ATT_WIDTH), F32),
                        pltpu.VMEM((ATT_HEADS, bs, bs), F32),
                        pltpu.VMEM((LANES, bs), F32)] + _flash_scratch(bs, s),
        compiler_params=_compiler_params(2),
        name="moba",
    )(p3, p3, p3, p3)


def _mlstm_kernel(qk_ref, v_ref, og_ref, z_ref, w_ref, cc_ref, cr_ref, rr_ref, hg_ref, y_ref,
                  cbuf_ref, cst_ref, mst_ref, fprev_ref, *, L):
    c = pl.program_id(1)
    halo = SUBLANES

    @pl.when(c == 0)
    def _():
        cbuf_ref[0:halo, :] = jnp.zeros((halo, cbuf_ref.shape[1]), F32)
        cst_ref[...] = jnp.zeros_like(cst_ref)
        mst_ref[...] = jnp.zeros_like(mst_ref)
        fprev_ref[...] = jnp.zeros_like(fprev_ref)

    @pl.when(c > 0)
    def _():
        cbuf_ref[0:halo, :] = cbuf_ref[L:L + halo, :]

    cbuf_ref[halo:halo + L, :] = qk_ref[...].astype(F32)
    conv = None
    for j in range(ML_CONV):
        off = halo - (ML_CONV - 1) + j
        term = cbuf_ref[off:off + L, :] * w_ref[j:j + 1, :]
        conv = term if conv is None else conv + term
    qk = conv * _sigmoid(conv)
    qk_w = ML_HEADS * HEAD_DIM
    q_b = qk[:, :qk_w].astype(BF16)
    k_s = qk[:, qk_w:] * ATT_SCALE
    k_b = k_s.astype(BF16)
    k_t = k_s.T

    row = lax.broadcasted_iota(jnp.int32, (L, L), 0)
    col = lax.broadcasted_iota(jnp.int32, (L, L), 1)
    tri = row >= col
    lane1 = lax.broadcasted_iota(jnp.int32, (1, LANES), 1)
    ones_col = jnp.broadcast_to(jnp.where(lane1 == 0, 1.0, 0.0), (L, LANES)).astype(BF16)

    for h in range(ML_HEADS):
        hp, hh = divmod(h, 2)
        lanes = slice(hp * LANES, (hp + 1) * LANES)
        qh = jnp.where(_head_lane_mask(hh), q_b[:, lanes], jnp.zeros((L, LANES), BF16))
        f_c = cc_ref[:, LANE_ML_F + h:LANE_ML_F + h + 1]
        g_r = cr_ref[LANE_ML_F + h:LANE_ML_F + h + 1, :] - rr_ref[LANE_ML_I + h:LANE_ML_I + h + 1, :]
        f0 = fprev_ref[0:1, LANE_ML_F + h:LANE_ML_F + h + 1]
        m_prev = mst_ref[h, 0:1, 0:1]

        d_intra = jnp.where(tri, f_c - g_r, NEG)
        d_inter = f_c - f0 + m_prev
        m_t = jnp.maximum(d_inter, jnp.max(d_intra, axis=1, keepdims=True))
        w_intra = jnp.exp(d_intra - m_t)
        w_inter = jnp.exp(d_inter - m_t)
        s_mat = (_dot_nt(qh, k_b[:, lanes]) * w_intra).astype(BF16)
        v_aug = jnp.concatenate([v_ref[:, h * ML_V_DIM:(h + 1) * ML_V_DIM], ones_col], axis=1)
        c_prev = cst_ref[hp]
        num = _dot(s_mat, v_aug) + w_inter * _dot(qh, c_prev.astype(BF16))
        den = num[:, ML_V_DIM:ML_V_DIM + 1]
        hv = num[:, :ML_V_DIM] / jnp.maximum(jnp.abs(den), jnp.exp(-m_t))

        f_last = f_c[L - 1:L, :]
        d_state = f_last - g_r
        m_new = jnp.maximum(f_last - f0 + m_prev, jnp.max(d_state, axis=1, keepdims=True))
        w_prev = jnp.exp(f_last - f0 + m_prev - m_new)
        w_s = jnp.exp(d_state - m_new)
        rows = slice(hh * HEAD_DIM, (hh + 1) * HEAD_DIM)
        kw = (k_t[h * HEAD_DIM:(h + 1) * HEAD_DIM, :] * w_s).astype(BF16)
        cst_ref[hp, rows, :] = w_prev * c_prev[rows, :] + _dot(kw, v_aug)
        mst_ref[h] = jnp.broadcast_to(m_new, mst_ref.shape[1:])

        vl = slice(h * ML_V_DIM, (h + 1) * ML_V_DIM)
        ho = hv * _sigmoid(og_ref[:, vl].astype(F32))
        yn = ho * lax.rsqrt(jnp.mean(ho * ho, axis=1, keepdims=True) + RMS_EPS) * hg_ref[:, vl]
        z = z_ref[:, vl].astype(F32)
        y_ref[:, vl] = (yn * (z * _sigmoid(z))).astype(BF16)

    fprev_ref[...] = cc_ref[L - 1:L, :]


def _mlstm(p3, conv_w, cs, cst, rawt, head_g, *, L=256):
    b, s, _ = p3.shape
    w512 = ML_HEADS * ML_V_DIM
    tile = lambda c: pl.BlockSpec((None, L, w512), lambda bi, i, c=c: (bi, i, c))
    rowt = pl.BlockSpec((None, GATE_ROWS, L), lambda bi, i: (bi, 0, i))
    return pl.pallas_call(
        functools.partial(_mlstm_kernel, L=L),
        grid=(b, s // L),
        in_specs=[tile(COL_C + 0), tile(COL_C + 1), tile(COL_C + 2), tile(COL_C + 3),
                  pl.BlockSpec((ML_CONV, w512), lambda bi, i: (0, 0)),
                  pl.BlockSpec((None, L, LANES), lambda bi, i: (bi, i, 0)),
                  rowt, rowt,
                  pl.BlockSpec((1, w512), lambda bi, i: (0, 0))],
        out_specs=pl.BlockSpec((None, L, w512), lambda bi, i: (bi, i, 0)),
        out_shape=jax.ShapeDtypeStruct((b, s, w512), BF16),
        scratch_shapes=[pltpu.VMEM((L + SUBLANES, w512), F32),
                        pltpu.VMEM((ML_HEADS // 2, 2 * HEAD_DIM, 2 * ML_V_DIM), F32),
                        pltpu.VMEM((ML_HEADS, SUBLANES, LANES), F32),
                        pltpu.VMEM((1, LANES), F32)],
        compiler_params=_compiler_params(2),
        name="mlstm",
    )(p3, p3, p3, p3, conv_w, cs, cst, rawt, head_g)


def _merge_kernel(ya_ref, yb_ref, yc_ref, g_ref, x_ref, wb_ref, wo_ref, fg_ref, o_ref, *, final):
    d = x_ref.shape[1]
    merged = None
    for n, y_ref in enumerate((ya_ref, yb_ref, yc_ref)):
        gate = _sigmoid(g_ref[:, n * d:(n + 1) * d].astype(F32))
        term = gate * _dot(y_ref[...], wb_ref[n])
        merged = term if merged is None else merged + term
    out = x_ref[...] + _dot(merged.astype(BF16), wo_ref[...])
    if final:
        out = out * lax.rsqrt(jnp.mean(out * out, axis=-1, keepdims=True) + RMS_EPS) * fg_ref[...]
    o_ref[...] = out


def _merge(ya, yb, yc, p2, x2, wb, wo, fg, *, final, tm=512):
    t, d = x2.shape
    w = ya.shape[1]
    ytile = pl.BlockSpec((tm, w), lambda i: (i, 0))
    return pl.pallas_call(
        functools.partial(_merge_kernel, final=final),
        grid=(t // tm,),
        in_specs=[ytile, ytile, ytile,
                  pl.BlockSpec((tm, N_BRANCHES * d), lambda i: (i, GATE_COL0 // (N_BRANCHES * d))),
                  pl.BlockSpec((tm, d), lambda i: (i, 0)),
                  pl.BlockSpec((N_BRANCHES, w, d), lambda i: (0, 0, 0)),
                  pl.BlockSpec((d, d), lambda i: (0, 0)),
                  pl.BlockSpec((1, d), lambda i: (0, 0))],
        out_specs=pl.BlockSpec((tm, d), lambda i: (i, 0)),
        out_shape=jax.ShapeDtypeStruct((t, d), F32),
        compiler_params=_compiler_params(1),
        name="merge_final" if final else "merge",
    )(ya, yb, yc, p2, x2, wb, wo, fg)


def _regroup_in_weights(w):
    aw = 4 * ATT_WIDTH
    a_f0 = aw
    b0 = a_f0 + ATT_HEADS
    c0 = b0 + aw
    c_i0 = c0 + 4 * ML_HEADS * ML_V_DIM
    g0 = c_i0 + 2 * ML_HEADS
    main = jnp.concatenate([w[:, :a_f0], w[:, b0:c_i0], w[:, g0:]], axis=1)
    small = jnp.concatenate([w[:, a_f0:b0], w[:, c_i0:g0]], axis=1)
    small = jnp.pad(small, ((0, 0), (0, LANES - small.shape[1])))
    return main, small


def kernel(x, norm_g, w_in, fox_b_f, mlstm_conv_w, mlstm_b_i, mlstm_b_f, mlstm_head_g, w_branch, w_out,
           final_norm_g):
    b, s, d = x.shape
    depth = w_in.shape[0]
    x2 = x.reshape(b * s, d)
    fg = final_norm_g.reshape(1, d)
    for layer in range(depth):
        main, small = _regroup_in_weights(w_in[layer])
        assert main.shape[1] == MAIN_WIDTH
        wm = main.astype(BF16)
        wsh = small.astype(BF16)
        wsl = (small - wsh.astype(F32)).astype(BF16)
        bias = jnp.concatenate([fox_b_f[layer], mlstm_b_i[layer], mlstm_b_f[layer]])
        bias = jnp.pad(bias, (0, LANES - bias.shape[0])).reshape(1, LANES)

        p2, gs = _inproj(x2, norm_g[layer].reshape(1, d), wm, wsh, wsl)
        p3 = p2.reshape(b, s, MAIN_WIDTH)
        cs, cst, rawt = _gates(gs.reshape(b, s, LANES), bias)
        ya = _fox(p3, cs, cst)
        yb = _moba(p3)
        yc = _mlstm(p3, mlstm_conv_w[layer], cs, cst, rawt, mlstm_head_g[layer].reshape(1, -1))
        w512 = ya.shape[-1]
        x2 = _merge(ya.reshape(b * s, w512), yb.reshape(b * s, w512), yc.reshape(b * s, w512), p2, x2,
                    w_branch[layer].astype(BF16), w_out[layer].astype(BF16), fg,
                    final=(layer == depth - 1))
    return x2.reshape(b, s, d)
```

```python
import functools

import numpy as np
import jax
import jax.numpy as jnp
from jax import lax
from jax.experimental import pallas as pl
from jax.experimental.pallas import tpu as pltpu

F32 = jnp.float32
BF16 = jnp.bfloat16

D_MODEL = 1024
HEAD_DIM = 64
ATT_HEADS = 8
ATT_WIDTH = ATT_HEADS * HEAD_DIM
MOBA_BLOCK = 256
MOBA_TOPK = 3
ML_HEADS = 4
ML_V_DIM = 128
ML_CONV = 4
N_BRANCHES = 3
RMS_EPS = 1e-6
NEG = -1e30
ATT_SCALE = HEAD_DIM ** -0.5
LOG2E = float(np.log2(np.e))
V_ROWS = 80

LANES = 128
SUBLANES = 8
VMEM_LIMIT_BYTES = 52 * 1024 * 1024

COL_A = 0
COL_B = 4
COL_C = 8
GATE_COL0 = 6144
MAIN_WIDTH = GATE_COL0 + N_BRANCHES * D_MODEL
LANE_FOX_F = 0
LANE_ML_I = 8
LANE_ML_F = 12
GATE_ROWS = 16

ALIBI_SLOPES = tuple(
    float(v) for v in 2.0 ** (-8.0 * (np.arange(ATT_HEADS, dtype=np.float32) + 1.0) / ATT_HEADS))


def _dot(a, b):
    return jnp.dot(a, b, preferred_element_type=F32)


def _dot_nt(a, b):
    return lax.dot_general(a, b, (((1,), (1,)), ((), ())), preferred_element_type=F32)


def _sigmoid(x):
    return 1.0 / (1.0 + jnp.exp(-x))


def _compiler_params(n_axes):
    return pltpu.CompilerParams(dimension_semantics=("arbitrary",) * n_axes,
                                vmem_limit_bytes=VMEM_LIMIT_BYTES)


def _inproj_kernel(x_ref, g_ref, wm_ref, wsh_ref, wsl_ref, p_ref, gs_ref, hn_ref):
    @pl.when(pl.program_id(1) == 0)
    def _():
        x = x_ref[...]
        y = x * lax.rsqrt(jnp.mean(x * x, axis=-1, keepdims=True) + RMS_EPS) * g_ref[...]
        hi = y.astype(BF16)
        lo = (y - hi.astype(F32)).astype(BF16)
        hn_ref[...] = hi
        gs_ref[...] = _dot(hi, wsh_ref[...]) + _dot(hi, wsl_ref[...]) + _dot(lo, wsh_ref[...])

    p_ref[...] = _dot(hn_ref[...], wm_ref[...]).astype(BF16)


def _inproj(x2, g, wm, wsh, wsl, *, tm=1024, tn=1536):
    t, d = x2.shape
    n = wm.shape[1]
    return pl.pallas_call(
        _inproj_kernel,
        grid=(t // tm, n // tn),
        in_specs=[pl.BlockSpec((tm, d), lambda i, j: (i, 0)),
                  pl.BlockSpec((1, d), lambda i, j: (0, 0)),
                  pl.BlockSpec((d, tn), lambda i, j: (0, j)),
                  pl.BlockSpec((d, LANES), lambda i, j: (0, 0)),
                  pl.BlockSpec((d, LANES), lambda i, j: (0, 0))],
        out_specs=[pl.BlockSpec((tm, tn), lambda i, j: (i, j)),
                   pl.BlockSpec((tm, LANES), lambda i, j: (i, 0))],
        out_shape=[jax.ShapeDtypeStruct((t, n), BF16),
                   jax.ShapeDtypeStruct((t, LANES), F32)],
        scratch_shapes=[pltpu.VMEM((tm, d), BF16)],
        compiler_params=_compiler_params(2),
        name="inproj",
    )(x2, g, wm, wsh, wsl)


def _gates_kernel(gs_ref, bias_ref, cs_ref, cst_ref, rawt_ref, *, cb):
    s = gs_ref.shape[0]
    lane = lax.broadcasted_iota(jnp.int32, (1, LANES), 1)
    is_log_i = (lane >= LANE_ML_I) & (lane < LANE_ML_F)
    r = lax.broadcasted_iota(jnp.int32, (cb, cb), 0)
    c = lax.broadcasted_iota(jnp.int32, (cb, cb), 1)
    tri = jnp.where(r >= c, 1.0, 0.0).astype(BF16)
    carry = jnp.zeros((1, LANES), F32)
    for blk in range(s // cb):
        rows = slice(blk * cb, (blk + 1) * cb)
        g = gs_ref[rows, :] + bias_ref[...]
        log_sig = jnp.minimum(g, 0.0) - jnp.log(1.0 + jnp.exp(-jnp.abs(g)))
        raw = jnp.where(is_log_i, g, log_sig)
        hi = raw.astype(BF16)
        r1 = raw - hi.astype(F32)
        mid = r1.astype(BF16)
        lo = (r1 - mid.astype(F32)).astype(BF16)
        cs = _dot(tri, hi) + _dot(tri, mid) + _dot(tri, lo) + carry
        carry = cs[cb - 1:cb, :]
        cs_ref[rows, :] = cs
        cst_ref[:, rows] = cs.T[:GATE_ROWS, :]
        rawt_ref[:, rows] = raw.T[:GATE_ROWS, :]


def _gates(gs3, bias, *, cb=256):
    b, s, _ = gs3.shape
    col = pl.BlockSpec((None, s, LANES), lambda i: (i, 0, 0))
    row = pl.BlockSpec((None, GATE_ROWS, s), lambda i: (i, 0, 0))
    return pl.pallas_call(
        functools.partial(_gates_kernel, cb=cb),
        grid=(b,),
        in_specs=[col, pl.BlockSpec((1, LANES), lambda i: (0, 0))],
        out_specs=[col, row, row],
        out_shape=[jax.ShapeDtypeStruct((b, s, LANES), F32),
                   jax.ShapeDtypeStruct((b, GATE_ROWS, s), F32),
                   jax.ShapeDtypeStruct((b, GATE_ROWS, s), F32)],
        compiler_params=_compiler_params(1),
        name="gates",
    )(gs3, bias)


def _split3(x):
    hi = x.astype(BF16).astype(F32)
    r = x - hi
    mid = r.astype(BF16).astype(F32)
    lo = (r - mid).astype(BF16).astype(F32)
    return hi, mid, lo


def _head_lane_mask(hh):
    lane = lax.broadcasted_iota(jnp.int32, (1, LANES), 1)
    return (lane >= HEAD_DIM * hh) & (lane < HEAD_DIM * (hh + 1))


def _store_qt(qt_ref, h, own_rows, aug_rows):
    pad = jnp.zeros((HEAD_DIM - aug_rows.shape[0], aug_rows.shape[1]), F32)
    other = jnp.concatenate([aug_rows, pad], axis=0)
    parts = [own_rows, other] if h % 2 == 0 else [other, own_rows]
    qt_ref[h] = jnp.concatenate(parts, axis=0).astype(BF16)


def _transpose_values(v_ref, vt_ref, blk):
    s = v_ref.shape[0]
    for b0 in range(0, s, blk):
        for hp in range(ATT_HEADS // 2):
            vb = v_ref[b0:b0 + blk, hp * LANES:(hp + 1) * LANES].astype(F32).T.astype(BF16)
            for hh in range(2):
                vt_ref[2 * hp + hh, 0:HEAD_DIM, b0:b0 + blk] = vb[hh * HEAD_DIM:(hh + 1) * HEAD_DIM, :]
    extra = V_ROWS - HEAD_DIM
    ones_row = jnp.where(lax.broadcasted_iota(jnp.int32, (extra, s), 0) == 0, 1.0, 0.0).astype(BF16)
    for h in range(ATT_HEADS):
        vt_ref[h, HEAD_DIM:V_ROWS, :] = ones_row


def _flash_all_heads(i, tq, kaug_ref, z_ref, o_ref, qt_ref, vt_ref):
    key = lax.broadcasted_iota(jnp.int32, (tq, tq), 0)
    qry = lax.broadcasted_iota(jnp.int32, (tq, tq), 1)
    causal = key <= qry

    def tile(start, diag, state):
        scores = []
        for h in range(ATT_HEADS):
            s = _dot(kaug_ref[h, pl.ds(start, tq), :], qt_ref[h])
            scores.append(jnp.where(causal, s, NEG) if diag else s)
        new_state = []
        for h in range(ATT_HEADS):
            vt = vt_ref[h, :, pl.ds(start, tq)]
            s = scores[h]
            if diag:
                m = jnp.max(s, axis=0, keepdims=True)
                acc = _dot(vt, jnp.exp2(s - m).astype(BF16))
            else:
                m_old, acc_old = state[h]
                m = jnp.maximum(m_old, jnp.max(s, axis=0, keepdims=True))
                acc = jnp.exp2(m_old - m) * acc_old + _dot(vt, jnp.exp2(s - m).astype(BF16))
            new_state.append((m, acc))
        return tuple(new_state)

    state = tile(pl.multiple_of(i * tq, tq), True, None)
    state = lax.fori_loop(0, i, lambda j, st: tile(pl.multiple_of(j * tq, tq), False, st), state)

    for hp in range(ATT_HEADS // 2):
        lanes = slice(hp * LANES, (hp + 1) * LANES)
        outs = [acc[:HEAD_DIM, :] / acc[HEAD_DIM:HEAD_DIM + 1, :] for _, acc in state[2 * hp:2 * hp + 2]]
        pair = jnp.concatenate(outs, axis=0).T
        z = z_ref[:, lanes].astype(F32)
        o_ref[:, lanes] = (pair * (z * _sigmoid(z))).astype(BF16)


def _flash_scratch(tq, s):
    return [pltpu.VMEM((ATT_HEADS, s, LANES), BF16),
            pltpu.VMEM((ATT_HEADS, LANES, tq), BF16),
            pltpu.VMEM((ATT_HEADS, V_ROWS, s), BF16)]


def _fox_key_placement():
    pm = np.zeros((3 * LANES, ATT_HEADS * LANES), np.float32)
    for h in range(ATT_HEADS):
        o = HEAD_DIM * (1 - h % 2)
        for p in range(3):
            pm[p * LANES + LANE_FOX_F + h, h * LANES + o + 3 + p] = -1.0
    return jnp.asarray(pm, BF16)


def _fox_kernel(q_ref, k_ref, v_ref, z_ref, cc_ref, cr_ref, pm_ref, o_ref, kaug_ref, qt_ref, vt_ref, *, tq):
    i = pl.program_id(1)
    lane = lax.broadcasted_iota(jnp.int32, (1, LANES), 1)

    @pl.when(i == 0)
    def _():
        _transpose_values(v_ref, vt_ref, tq)
        for b0 in range(0, k_ref.shape[0], tq):
            pieces = _split3(cc_ref[b0:b0 + tq, :] * LOG2E)
            aug = _dot(jnp.concatenate(pieces, axis=1).astype(BF16), pm_ref[...])
            for h in range(ATT_HEADS):
                hp, hh = divmod(h, 2)
                o = HEAD_DIM * (1 - hh)
                a = jnp.where((lane >= o) & (lane < o + 3), 1.0, aug[:, h * LANES:(h + 1) * LANES])
                kp = k_ref[b0:b0 + tq, hp * LANES:(hp + 1) * LANES]
                kaug_ref[h, b0:b0 + tq, :] = jnp.where(_head_lane_mask(hh), kp, a.astype(BF16))

    q0 = pl.multiple_of(i * tq, tq)
    c_hi, c_mid, c_lo = _split3(cr_ref[:, pl.ds(q0, tq)] * LOG2E)
    tail = jnp.concatenate([jnp.ones((3, tq), F32), jnp.zeros((2, tq), F32)], axis=0)
    for hp in range(ATT_HEADS // 2):
        qp = q_ref[:, hp * LANES:(hp + 1) * LANES].astype(F32).T
        for hh in range(2):
            h = 2 * hp + hh
            r = LANE_FOX_F + h
            aug_rows = jnp.concatenate([c_hi[r:r + 1], c_mid[r:r + 1], c_lo[r:r + 1], tail], axis=0)
            _store_qt(qt_ref, h, qp[hh * HEAD_DIM:(hh + 1) * HEAD_DIM, :], aug_rows)

    _flash_all_heads(i, tq, kaug_ref, z_ref, o_ref, qt_ref, vt_ref)


def _fox(p3, cs, cst, *, tq=256):
    b, s, _ = p3.shape
    tile = lambda c: pl.BlockSpec((None, tq, ATT_WIDTH), lambda bi, i, c=c: (bi, i, c))
    full = lambda c: pl.BlockSpec((None, s, ATT_WIDTH), lambda bi, i, c=c: (bi, 0, c))
    pm = _fox_key_placement()
    return pl.pallas_call(
        functools.partial(_fox_kernel, tq=tq),
        grid=(b, s // tq),
        in_specs=[tile(COL_A + 0), full(COL_A + 1), full(COL_A + 2), tile(COL_A + 3),
                  pl.BlockSpec((None, s, LANES), lambda bi, i: (bi, 0, 0)),
                  pl.BlockSpec((None, GATE_ROWS, s), lambda bi, i: (bi, 0, 0)),
                  pl.BlockSpec(pm.shape, lambda bi, i: (0, 0))],
        out_specs=pl.BlockSpec((None, tq, ATT_WIDTH), lambda bi, i: (bi, i, 0)),
        out_shape=jax.ShapeDtypeStruct((b, s, ATT_WIDTH), BF16),
        scratch_shapes=_flash_scratch(tq, s),
        compiler_params=_compiler_params(2),
        name="fox",
    )(p3, p3, p3, p3, cs, cst, pm)


def _split3_host(x):
    x = np.asarray(x, np.float32)
    hi = x.astype(BF16).astype(np.float32)
    mid = (x - hi).astype(BF16).astype(np.float32)
    lo = (x - hi - mid).astype(BF16).astype(np.float32)
    return hi, mid, lo


def _moba_constants(s, bs):
    slopes2 = np.asarray(ALIBI_SLOPES, np.float32) * np.float32(LOG2E)
    pos = np.arange(s)
    r_s = (pos % bs).astype(np.float32)
    block_onehot = (pos[:, None] // bs == np.arange(SUBLANES)[None, :]).astype(np.float32)
    r_t = np.arange(bs, dtype=np.float32)
    augk, augq = [], []
    for h in range(ATT_HEADS):
        k_pieces = np.stack(_split3_host(slopes2[h] * r_s), axis=1)
        half = np.concatenate([k_pieces, np.ones((s, 3), np.float32), np.zeros((s, 2), np.float32),
                               block_onehot, block_onehot, block_onehot,
                               np.zeros((s, HEAD_DIM - 4 * SUBLANES), np.float32)], axis=1)
        zeros = np.zeros((s, HEAD_DIM), np.float32)
        augk.append(np.concatenate([zeros, half] if h % 2 == 0 else [half, zeros], axis=1))
        q_pieces = np.stack(_split3_host(-slopes2[h] * r_t), axis=0)
        augq.append(np.concatenate([np.ones((3, bs), np.float32), q_pieces,
                                    np.zeros((2, bs), np.float32)], axis=0))
    return jnp.asarray(np.stack(augk), BF16), jnp.asarray(np.stack(augq), F32)


def _moba_kernel(q_ref, k_ref, v_ref, z_ref, augk_ref, augq_ref, o_ref, kmt_ref, kaug_ref, qt_ref, vt_ref):
    i = pl.program_id(1)
    bs = MOBA_BLOCK
    nb = k_ref.shape[0] // bs

    @pl.when(i == 0)
    def _():
        _transpose_values(v_ref, vt_ref, bs)
        for h in range(ATT_HEADS):
            hp, hh = divmod(h, 2)
            kaug_ref[h] = jnp.where(_head_lane_mask(hh), k_ref[:, hp * LANES:(hp + 1) * LANES], augk_ref[h])
        kms = [jnp.mean(k_ref[j * bs:(j + 1) * bs, :].astype(F32), axis=0, keepdims=True) for j in range(nb)]
        kms += [jnp.zeros((1, ATT_WIDTH), F32)] * (SUBLANES - nb)
        km_all = jnp.concatenate(kms, axis=0)
        head_of_lane = lax.broadcasted_iota(jnp.int32, (SUBLANES, ATT_WIDTH), 1) // HEAD_DIM
        kmt_ref[...] = jnp.zeros_like(kmt_ref)
        for h in range(ATT_HEADS):
            kmt_ref[h * SUBLANES:(h + 1) * SUBLANES, :] = jnp.where(head_of_lane == h, km_all, 0.0)

    gate = _dot_nt(q_ref[...], kmt_ref[...].astype(BF16))
    lane = lax.broadcasted_iota(jnp.int32, (bs, LANES), 1)
    blk = lane & (SUBLANES - 1)
    valid = (blk < i) & (lane < ATT_HEADS * SUBLANES)
    gate = jnp.where(valid, gate, NEG)
    rank = jnp.zeros((bs, LANES), F32)
    for r in range(1, SUBLANES):
        lower = blk >= r
        other = jnp.where(lower, pltpu.roll(gate, r, axis=1), pltpu.roll(gate, LANES - SUBLANES + r, axis=1))
        rank = rank + jnp.where(other > gate, 1.0, 0.0) + jnp.where((other == gate) & lower, 1.0, 0.0)
    slope_of_lane = jnp.zeros((bs, LANES), F32)
    for h in range(ATT_HEADS):
        slope_of_lane = jnp.where((lane >> 3) == h, ALIBI_SLOPES[h] * LOG2E, slope_of_lane)
    keep = (valid & (rank < MOBA_TOPK)) | (blk == i)
    bias = jnp.where(keep, -(slope_of_lane * bs) * (i - blk).astype(F32), NEG)
    bias_t = [p.T for p in _split3(bias)]

    for hp in range(ATT_HEADS // 2):
        qp = q_ref[:, hp * LANES:(hp + 1) * LANES].astype(F32).T
        for hh in range(2):
            h = 2 * hp + hh
            rows = slice(h * SUBLANES, (h + 1) * SUBLANES)
            aug_rows = jnp.concatenate([augq_ref[h]] + [p[rows, :] for p in bias_t], axis=0)
            _store_qt(qt_ref, h, qp[hh * HEAD_DIM:(hh + 1) * HEAD_DIM, :], aug_rows)

    _flash_all_heads(i, bs, kaug_ref, z_ref, o_ref, qt_ref, vt_ref)


def _moba(p3):
    b, s, _ = p3.shape
    bs = MOBA_BLOCK
    assert s % bs == 0 and s // bs <= SUBLANES
    augk, augq = _moba_constants(s, bs)
    tile = lambda c: pl.BlockSpec((None, bs, ATT_WIDTH), lambda bi, i, c=c: (bi, i, c))
    full = lambda c: pl.BlockSpec((None, s, ATT_WIDTH), lambda bi, i, c=c: (bi, 0, c))
    return pl.pallas_call(
        _moba_kernel,
        grid=(b, s // bs),
        in_specs=[tile(COL_B + 0), full(COL_B + 1), full(COL_B + 2), tile(COL_B + 3),
                  pl.BlockSpec(augk.shape, lambda bi, i: (0, 0, 0)),
                  pl.BlockSpec(augq.shape, lambda bi, i: (0, 0, 0))],
        out_specs=pl.BlockSpec((None, bs, ATT_WIDTH), lambda bi, i: (bi, i, 0)),
        out_shape=jax.ShapeDtypeStruct((b, s, ATT_WIDTH), BF16),
        scratch_shapes=[pltpu.VMEM((LANES, ATT_WIDTH), F32)] + _flash_scratch(bs, s),
        compiler_params=_compiler_params(2),
        name="moba",
    )(p3, p3, p3, p3, augk, augq)


def _mlstm_kernel(qk_ref, v_ref, og_ref, z_ref, w_ref, cc_ref, cr_ref, rr_ref, hg_ref, y_ref,
                  cbuf_ref, cst_ref, mst_ref, fprev_ref, *, L):
    c = pl.program_id(1)
    halo = SUBLANES

    @pl.when(c == 0)
    def _():
        cbuf_ref[0:halo, :] = jnp.zeros((halo, cbuf_ref.shape[1]), F32)
        cst_ref[...] = jnp.zeros_like(cst_ref)
        mst_ref[...] = jnp.zeros_like(mst_ref)
        fprev_ref[...] = jnp.zeros_like(fprev_ref)

    @pl.when(c > 0)
    def _():
        cbuf_ref[0:halo, :] = cbuf_ref[L:L + halo, :]

    cbuf_ref[halo:halo + L, :] = qk_ref[...].astype(F32)
    conv = None
    for j in range(ML_CONV):
        off = halo - (ML_CONV - 1) + j
        term = cbuf_ref[off:off + L, :] * w_ref[j:j + 1, :]
        conv = term if conv is None else conv + term
    qk = conv * _sigmoid(conv)
    qk_w = ML_HEADS * HEAD_DIM
    q_b = qk[:, :qk_w].astype(BF16)
    k_s = qk[:, qk_w:] * ATT_SCALE
    k_b = k_s.astype(BF16)
    k_t = k_s.T

    row = lax.broadcasted_iota(jnp.int32, (L, L), 0)
    col = lax.broadcasted_iota(jnp.int32, (L, L), 1)
    tri = row >= col
    lane1 = lax.broadcasted_iota(jnp.int32, (1, LANES), 1)
    ones_col = jnp.broadcast_to(jnp.where(lane1 == 0, 1.0, 0.0), (L, LANES)).astype(BF16)

    for h in range(ML_HEADS):
        hp, hh = divmod(h, 2)
        lanes = slice(hp * LANES, (hp + 1) * LANES)
        qh = jnp.where(_head_lane_mask(hh), q_b[:, lanes], jnp.zeros((L, LANES), BF16))
        f_c = cc_ref[:, LANE_ML_F + h:LANE_ML_F + h + 1]
        g_r = cr_ref[LANE_ML_F + h:LANE_ML_F + h + 1, :] - rr_ref[LANE_ML_I + h:LANE_ML_I + h + 1, :]
        f0 = fprev_ref[0:1, LANE_ML_F + h:LANE_ML_F + h + 1]
        m_prev = mst_ref[h, 0:1, 0:1]

        d_intra = jnp.where(tri, f_c - g_r, NEG)
        d_inter = f_c - f0 + m_prev
        m_t = jnp.maximum(d_inter, jnp.max(d_intra, axis=1, keepdims=True))
        w_intra = jnp.exp(d_intra - m_t)
        w_inter = jnp.exp(d_inter - m_t)
        s_mat = (_dot_nt(qh, k_b[:, lanes]) * w_intra).astype(BF16)
        v_aug = jnp.concatenate([v_ref[:, h * ML_V_DIM:(h + 1) * ML_V_DIM], ones_col], axis=1)
        c_prev = cst_ref[hp]
        num = _dot(s_mat, v_aug) + w_inter * _dot(qh, c_prev.astype(BF16))
        den = num[:, ML_V_DIM:ML_V_DIM + 1]
        hv = num[:, :ML_V_DIM] / jnp.maximum(jnp.abs(den), jnp.exp(-m_t))

        f_last = f_c[L - 1:L, :]
        d_state = f_last - g_r
        m_new = jnp.maximum(f_last - f0 + m_prev, jnp.max(d_state, axis=1, keepdims=True))
        w_prev = jnp.exp(f_last - f0 + m_prev - m_new)
        w_s = jnp.exp(d_state - m_new)
        rows = slice(hh * HEAD_DIM, (hh + 1) * HEAD_DIM)
        kw = (k_t[h * HEAD_DIM:(h + 1) * HEAD_DIM, :] * w_s).astype(BF16)
        cst_ref[hp, rows, :] = w_prev * c_prev[rows, :] + _dot(kw, v_aug)
        mst_ref[h] = jnp.broadcast_to(m_new, mst_ref.shape[1:])

        vl = slice(h * ML_V_DIM, (h + 1) * ML_V_DIM)
        ho = hv * _sigmoid(og_ref[:, vl].astype(F32))
        yn = ho * lax.rsqrt(jnp.mean(ho * ho, axis=1, keepdims=True) + RMS_EPS) * hg_ref[:, vl]
        z = z_ref[:, vl].astype(F32)
        y_ref[:, vl] = (yn * (z * _sigmoid(z))).astype(BF16)

    fprev_ref[...] = cc_ref[L - 1:L, :]


def _mlstm(p3, conv_w, cs, cst, rawt, head_g, *, L=256):
    b, s, _ = p3.shape
    w512 = ML_HEADS * ML_V_DIM
    tile = lambda c: pl.BlockSpec((None, L, w512), lambda bi, i, c=c: (bi, i, c))
    rowt = pl.BlockSpec((None, GATE_ROWS, L), lambda bi, i: (bi, 0, i))
    return pl.pallas_call(
        functools.partial(_mlstm_kernel, L=L),
        grid=(b, s // L),
        in_specs=[tile(COL_C + 0), tile(COL_C + 1), tile(COL_C + 2), tile(COL_C + 3),
                  pl.BlockSpec((ML_CONV, w512), lambda bi, i: (0, 0)),
                  pl.BlockSpec((None, L, LANES), lambda bi, i: (bi, i, 0)),
                  rowt, rowt,
                  pl.BlockSpec((1, w512), lambda bi, i: (0, 0))],
        out_specs=pl.BlockSpec((None, L, w512), lambda bi, i: (bi, i, 0)),
        out_shape=jax.ShapeDtypeStruct((b, s, w512), BF16),
        scratch_shapes=[pltpu.VMEM((L + SUBLANES, w512), F32),
                        pltpu.VMEM((ML_HEADS // 2, 2 * HEAD_DIM, 2 * ML_V_DIM), F32),
                        pltpu.VMEM((ML_HEADS, SUBLANES, LANES), F32),
                        pltpu.VMEM((1, LANES), F32)],
        compiler_params=_compiler_params(2),
        name="mlstm",
    )(p3, p3, p3, p3, conv_w, cs, cst, rawt, head_g)


def _merge_kernel(ya_ref, yb_ref, yc_ref, g_ref, x_ref, wb_ref, wo_ref, fg_ref, o_ref, *, final):
    d = x_ref.shape[1]
    merged = None
    for n, y_ref in enumerate((ya_ref, yb_ref, yc_ref)):
        gate = _sigmoid(g_ref[:, n * d:(n + 1) * d].astype(F32))
        term = gate * _dot(y_ref[...], wb_ref[n])
        merged = term if merged is None else merged + term
    out = x_ref[...] + _dot(merged.astype(BF16), wo_ref[...])
    if final:
        out = out * lax.rsqrt(jnp.mean(out * out, axis=-1, keepdims=True) + RMS_EPS) * fg_ref[...]
    o_ref[...] = out


def _merge(ya, yb, yc, p2, x2, wb, wo, fg, *, final, tm=512):
    t, d = x2.shape
    w = ya.shape[1]
    ytile = pl.BlockSpec((tm, w), lambda i: (i, 0))
    return pl.pallas_call(
        functools.partial(_merge_kernel, final=final),
        grid=(t // tm,),
        in_specs=[ytile, ytile, ytile,
                  pl.BlockSpec((tm, N_BRANCHES * d), lambda i: (i, GATE_COL0 // (N_BRANCHES * d))),
                  pl.BlockSpec((tm, d), lambda i: (i, 0)),
                  pl.BlockSpec((N_BRANCHES, w, d), lambda i: (0, 0, 0)),
                  pl.BlockSpec((d, d), lambda i: (0, 0)),
                  pl.BlockSpec((1, d), lambda i: (0, 0))],
        out_specs=pl.BlockSpec((tm, d), lambda i: (i, 0)),
        out_shape=jax.ShapeDtypeStruct((t, d), F32),
        compiler_params=_compiler_params(1),
        name="merge_final" if final else "merge",
    )(ya, yb, yc, p2, x2, wb, wo, fg)


def _regroup_in_weights(w):
    aw = 4 * ATT_WIDTH
    a_f0 = aw
    b0 = a_f0 + ATT_HEADS
    c0 = b0 + aw
    c_i0 = c0 + 4 * ML_HEADS * ML_V_DIM
    g0 = c_i0 + 2 * ML_HEADS
    qscale = ATT_SCALE * LOG2E
    main = jnp.concatenate([w[:, :ATT_WIDTH] * qscale, w[:, ATT_WIDTH:a_f0],
                            w[:, b0:b0 + ATT_WIDTH] * qscale, w[:, b0 + ATT_WIDTH:c_i0], w[:, g0:]], axis=1)
    small = jnp.concatenate([w[:, a_f0:b0], w[:, c_i0:g0]], axis=1)
    small = jnp.pad(small, ((0, 0), (0, LANES - small.shape[1])))
    return main, small


def kernel(x, norm_g, w_in, fox_b_f, mlstm_conv_w, mlstm_b_i, mlstm_b_f, mlstm_head_g, w_branch, w_out,
           final_norm_g):
    b, s, d = x.shape
    depth = w_in.shape[0]
    x2 = x.reshape(b * s, d)
    fg = final_norm_g.reshape(1, d)
    for layer in range(depth):
        main, small = _regroup_in_weights(w_in[layer])
        assert main.shape[1] == MAIN_WIDTH
        wm = main.astype(BF16)
        wsh = small.astype(BF16)
        wsl = (small - wsh.astype(F32)).astype(BF16)
        bias = jnp.concatenate([fox_b_f[layer], mlstm_b_i[layer], mlstm_b_f[layer]])
        bias = jnp.pad(bias, (0, LANES - bias.shape[0])).reshape(1, LANES)

        p2, gs = _inproj(x2, norm_g[layer].reshape(1, d), wm, wsh, wsl)
        p3 = p2.reshape(b, s, MAIN_WIDTH)
        cs, cst, rawt = _gates(gs.reshape(b, s, LANES), bias)
        ya = _fox(p3, cs, cst)
        yb = _moba(p3)
        yc = _mlstm(p3, mlstm_conv_w[layer], cs, cst, rawt, mlstm_head_g[layer].reshape(1, -1))
        w512 = ya.shape[-1]
        x2 = _merge(ya.reshape(b * s, w512), yb.reshape(b * s, w512), yc.reshape(b * s, w512), p2, x2,
                    w_branch[layer].astype(BF16), w_out[layer].astype(BF16), fg,
                    final=(layer == depth - 1))
    return x2.reshape(b, s, d)
```

```python
import functools

import numpy as np
import jax
import jax.numpy as jnp
from jax import lax
from jax.experimental import pallas as pl
from jax.experimental.pallas import tpu as pltpu

F32 = jnp.float32
BF16 = jnp.bfloat16

D_MODEL = 1024
HEAD_DIM = 64
ATT_HEADS = 8
ATT_WIDTH = ATT_HEADS * HEAD_DIM
MOBA_BLOCK = 256
MOBA_TOPK = 3
ML_HEADS = 4
ML_V_DIM = 128
ML_CONV = 4
N_BRANCHES = 3
RMS_EPS = 1e-6
NEG = -1e30
ATT_SCALE = HEAD_DIM ** -0.5
LOG2E = float(np.log2(np.e))
V_ROWS = 80

LANES = 128
SUBLANES = 8
VMEM_LIMIT_BYTES = 52 * 1024 * 1024

COL_A = 0
COL_B = 4
COL_C = 8
GATE_COL0 = 6144
MAIN_WIDTH = GATE_COL0 + N_BRANCHES * D_MODEL
LANE_FOX_F = 0
LANE_ML_I = 8
LANE_ML_F = 12
GATE_ROWS = 16

ALIBI_SLOPES = tuple(
    float(v) for v in 2.0 ** (-8.0 * (np.arange(ATT_HEADS, dtype=np.float32) + 1.0) / ATT_HEADS))


def _dot(a, b):
    return jnp.dot(a, b, preferred_element_type=F32)


def _dot_nt(a, b):
    return lax.dot_general(a, b, (((1,), (1,)), ((), ())), preferred_element_type=F32)


def _sigmoid(x):
    return 1.0 / (1.0 + jnp.exp(-x))


def _compiler_params(n_axes):
    return pltpu.CompilerParams(dimension_semantics=("arbitrary",) * n_axes,
                                vmem_limit_bytes=VMEM_LIMIT_BYTES)


def _inproj_kernel(x_ref, g_ref, wa_ref, wbc_ref, wg_ref, ws_ref, p_ref, gs_ref, hn_ref, *, n_a, n_bc, rc):
    j = pl.program_id(1)
    tm = x_ref.shape[0]

    @pl.when(j == 0)
    def _():
        ws = ws_ref[...]
        wsh = ws.astype(BF16)
        ws2 = jnp.concatenate([wsh, (ws - wsh.astype(F32)).astype(BF16)], axis=1)
        for r0 in range(0, tm, rc):
            rows = slice(r0, r0 + rc)
            x = x_ref[rows, :]
            y = x * lax.rsqrt(jnp.mean(x * x, axis=-1, keepdims=True) + RMS_EPS) * g_ref[...]
            hi = y.astype(BF16)
            hn_ref[rows, :] = hi
            gs2 = _dot(hi, ws2)
            gs_ref[rows, :] = gs2[:, :LANES] + gs2[:, LANES:]
            p_ref[rows, :] = _dot(hi, wa_ref[...]).astype(BF16)

    @pl.when((j > 0) & (j < n_a))
    def _():
        p_ref[...] = _dot(hn_ref[...], wa_ref[...]).astype(BF16)

    @pl.when((j >= n_a) & (j < n_a + n_bc))
    def _():
        p_ref[...] = _dot(hn_ref[...], wbc_ref[...]).astype(BF16)

    @pl.when(j >= n_a + n_bc)
    def _():
        p_ref[...] = _dot(hn_ref[...], wg_ref[...]).astype(BF16)


def _inproj(x2, g, wa, wbc, wg, ws, *, tm=2048, tn=1024, rc=256):
    t, d = x2.shape
    tm = min(tm, t)
    n_a, n_bc, n_g = wa.shape[1] // tn, wbc.shape[1] // tn, wg.shape[1] // tn
    n = (n_a + n_bc + n_g) * tn
    return pl.pallas_call(
        functools.partial(_inproj_kernel, n_a=n_a, n_bc=n_bc, rc=rc),
        grid=(t // tm, n_a + n_bc + n_g),
        in_specs=[pl.BlockSpec((tm, d), lambda i, j: (i, 0)),
                  pl.BlockSpec((1, d), lambda i, j: (0, 0)),
                  pl.BlockSpec((d, tn), lambda i, j: (0, jnp.minimum(j, n_a - 1))),
                  pl.BlockSpec((d, tn), lambda i, j: (0, jnp.clip(j - n_a, 0, n_bc - 1))),
                  pl.BlockSpec((d, tn), lambda i, j: (0, jnp.clip(j - n_a - n_bc, 0, n_g - 1))),
                  pl.BlockSpec((d, LANES), lambda i, j: (0, 0))],
        out_specs=[pl.BlockSpec((tm, tn), lambda i, j: (i, j)),
                   pl.BlockSpec((tm, LANES), lambda i, j: (i, 0))],
        out_shape=[jax.ShapeDtypeStruct((t, n), BF16),
                   jax.ShapeDtypeStruct((t, LANES), F32)],
        scratch_shapes=[pltpu.VMEM((tm, d), BF16)],
        compiler_params=_compiler_params(2),
        name="inproj",
    )(x2, g, wa, wbc, wg, ws)


def _gates_kernel(gs_ref, bias_ref, cs_ref, cst_ref, rawt_ref, *, cb):
    s = gs_ref.shape[0]
    lane = lax.broadcasted_iota(jnp.int32, (1, LANES), 1)
    is_log_i = (lane >= LANE_ML_I) & (lane < LANE_ML_F)
    r = lax.broadcasted_iota(jnp.int32, (cb, cb), 0)
    c = lax.broadcasted_iota(jnp.int32, (cb, cb), 1)
    tri = jnp.where(r >= c, 1.0, 0.0).astype(BF16)
    carry = jnp.zeros((1, LANES), F32)
    for blk in range(s // cb):
        rows = slice(blk * cb, (blk + 1) * cb)
        g = gs_ref[rows, :] + bias_ref[...]
        log_sig = jnp.minimum(g, 0.0) - jnp.log(1.0 + jnp.exp(-jnp.abs(g)))
        raw = jnp.where(is_log_i, g, log_sig)
        hi = raw.astype(BF16)
        r1 = raw - hi.astype(F32)
        mid = r1.astype(BF16)
        lo = (r1 - mid.astype(F32)).astype(BF16)
        cs = _dot(tri, hi) + _dot(tri, mid) + _dot(tri, lo) + carry
        carry = cs[cb - 1:cb, :]
        cs_ref[rows, :] = cs
        cst_ref[:, rows] = cs.T[:GATE_ROWS, :]
        rawt_ref[:, rows] = raw.T[:GATE_ROWS, :]


def _gates(gs3, bias, *, cb=256):
    b, s, _ = gs3.shape
    col = pl.BlockSpec((None, s, LANES), lambda i: (i, 0, 0))
    row = pl.BlockSpec((None, GATE_ROWS, s), lambda i: (i, 0, 0))
    return pl.pallas_call(
        functools.partial(_gates_kernel, cb=cb),
        grid=(b,),
        in_specs=[col, pl.BlockSpec((1, LANES), lambda i: (0, 0))],
        out_specs=[col, row, row],
        out_shape=[jax.ShapeDtypeStruct((b, s, LANES), F32),
                   jax.ShapeDtypeStruct((b, GATE_ROWS, s), F32),
                   jax.ShapeDtypeStruct((b, GATE_ROWS, s), F32)],
        compiler_params=_compiler_params(1),
        name="gates",
    )(gs3, bias)


def _split3(x):
    hi = x.astype(BF16).astype(F32)
    r = x - hi
    mid = r.astype(BF16).astype(F32)
    lo = (r - mid).astype(BF16).astype(F32)
    return hi, mid, lo


def _split3_host(x):
    x = np.asarray(x, np.float32)
    hi = x.astype(BF16).astype(np.float32)
    mid = (x - hi).astype(BF16).astype(np.float32)
    lo = (x - hi - mid).astype(BF16).astype(np.float32)
    return hi, mid, lo


def _head_lane_mask(hh):
    lane = lax.broadcasted_iota(jnp.int32, (1, LANES), 1)
    return (lane >= HEAD_DIM * hh) & (lane < HEAD_DIM * (hh + 1))


def _store_qt(qt_ref, h, q_pair_t, aug_rows):
    chan = lax.broadcasted_iota(jnp.int32, (LANES, 1), 0)
    hh = h % 2
    own = (chan >= HEAD_DIM * hh) & (chan < HEAD_DIM * (hh + 1))
    pad = jnp.zeros((LANES - aug_rows.shape[0], aug_rows.shape[1]), F32)
    qt_ref[h] = jnp.concatenate([jnp.where(own, q_pair_t, 0.0), aug_rows, pad], axis=0).astype(BF16)


def _transpose_values(v_ref, vt_ref, blk):
    s = v_ref.shape[0]
    for b0 in range(0, s, blk):
        for hp in range(ATT_HEADS // 2):
            vb = v_ref[b0:b0 + blk, hp * LANES:(hp + 1) * LANES].astype(F32).T.astype(BF16)
            for hh in range(2):
                vt_ref[2 * hp + hh, 0:HEAD_DIM, b0:b0 + blk] = vb[hh * HEAD_DIM:(hh + 1) * HEAD_DIM, :]
    extra = V_ROWS - HEAD_DIM
    ones_row = jnp.where(lax.broadcasted_iota(jnp.int32, (extra, s), 0) == 0, 1.0, 0.0).astype(BF16)
    for h in range(ATT_HEADS):
        vt_ref[h, HEAD_DIM:V_ROWS, :] = ones_row


def _flash_all_heads(i, tq, k_ref, aug_ref, z_ref, o_ref, qt_ref, vt_ref):
    key = lax.broadcasted_iota(jnp.int32, (tq, tq), 0)
    qry = lax.broadcasted_iota(jnp.int32, (tq, tq), 1)
    causal = key <= qry

    def tile(start, diag, state):
        aug = aug_ref[pl.ds(start, tq), :]
        scores = []
        for hp in range(ATT_HEADS // 2):
            kj = jnp.concatenate([k_ref[pl.ds(start, tq), hp * LANES:(hp + 1) * LANES], aug], axis=1)
            for h in (2 * hp, 2 * hp + 1):
                s = _dot(kj, qt_ref[h])
                scores.append(jnp.where(causal, s, NEG) if diag else s)
        new_state = []
        for h in range(ATT_HEADS):
            vt = vt_ref[h, :, pl.ds(start, tq)]
            s = scores[h]
            if diag:
                m = jnp.max(s, axis=0, keepdims=True)
                acc = _dot(vt, jnp.exp2(s - m).astype(BF16))
            else:
                m_old, acc_old = state[h]
                m = jnp.maximum(m_old, jnp.max(s, axis=0, keepdims=True))
                acc = jnp.exp2(m_old - m) * acc_old + _dot(vt, jnp.exp2(s - m).astype(BF16))
            new_state.append((m, acc))
        return tuple(new_state)

    state = tile(pl.multiple_of(i * tq, tq), True, None)
    state = lax.fori_loop(0, i, lambda j, st: tile(pl.multiple_of(j * tq, tq), False, st), state)

    for hp in range(ATT_HEADS // 2):
        lanes = slice(hp * LANES, (hp + 1) * LANES)
        outs = [acc[:HEAD_DIM, :] / acc[HEAD_DIM:HEAD_DIM + 1, :] for _, acc in state[2 * hp:2 * hp + 2]]
        pair = jnp.concatenate(outs, axis=0).T
        z = z_ref[:, lanes].astype(F32)
        o_ref[:, lanes] = (pair * (z * _sigmoid(z))).astype(BF16)


def _flash_scratch(tq, s):
    return [pltpu.VMEM((ATT_HEADS, 2 * LANES, tq), BF16),
            pltpu.VMEM((ATT_HEADS, V_ROWS, s), BF16)]


FOX_ONES0 = ATT_HEADS * SUBLANES


def _fox_key_placement():
    pm = np.zeros((3 * LANES, LANES), np.float32)
    for h in range(ATT_HEADS):
        for p in range(3):
            pm[p * LANES + LANE_FOX_F + h, SUBLANES * h + p] = -1.0
    return jnp.asarray(pm, BF16)


def _fox_kernel(q_ref, k_ref, v_ref, z_ref, cc_ref, cr_ref, pm_ref, o_ref, aug_ref, qt_ref, vt_ref, *, tq):
    i = pl.program_id(1)
    lane = lax.broadcasted_iota(jnp.int32, (1, LANES), 1)

    @pl.when(i == 0)
    def _():
        _transpose_values(v_ref, vt_ref, tq)
        for b0 in range(0, k_ref.shape[0], tq):
            pieces = _split3(cc_ref[b0:b0 + tq, :] * LOG2E)
            aug = _dot(jnp.concatenate(pieces, axis=1).astype(BF16), pm_ref[...])
            aug_ref[b0:b0 + tq, :] = jnp.where((lane >= FOX_ONES0) & (lane < FOX_ONES0 + 3), 1.0, aug).astype(BF16)

    q0 = pl.multiple_of(i * tq, tq)
    c_hi, c_mid, c_lo = _split3(cr_ref[:, pl.ds(q0, tq)] * LOG2E)
    row = lax.broadcasted_iota(jnp.int32, (FOX_ONES0, tq), 0)
    for hp in range(ATT_HEADS // 2):
        qp = q_ref[:, hp * LANES:(hp + 1) * LANES].astype(F32).T
        for h in (2 * hp, 2 * hp + 1):
            r = LANE_FOX_F + h
            ones_rows = jnp.where((row >= SUBLANES * h) & (row < SUBLANES * h + 3), 1.0, 0.0)
            aug_rows = jnp.concatenate([ones_rows, c_hi[r:r + 1], c_mid[r:r + 1], c_lo[r:r + 1]], axis=0)
            _store_qt(qt_ref, h, qp, aug_rows)

    _flash_all_heads(i, tq, k_ref, aug_ref, z_ref, o_ref, qt_ref, vt_ref)


def _fox(p3, cs, cst, *, tq=256):
    b, s, _ = p3.shape
    tile = lambda c: pl.BlockSpec((None, tq, ATT_WIDTH), lambda bi, i, c=c: (bi, i, c))
    full = lambda c: pl.BlockSpec((None, s, ATT_WIDTH), lambda bi, i, c=c: (bi, 0, c))
    pm = _fox_key_placement()
    return pl.pallas_call(
        functools.partial(_fox_kernel, tq=tq),
        grid=(b, s // tq),
        in_specs=[tile(COL_A + 0), full(COL_A + 1), full(COL_A + 2), tile(COL_A + 3),
                  pl.BlockSpec((None, s, LANES), lambda bi, i: (bi, 0, 0)),
                  pl.BlockSpec((None, GATE_ROWS, s), lambda bi, i: (bi, 0, 0)),
                  pl.BlockSpec(pm.shape, lambda bi, i: (0, 0))],
        out_specs=pl.BlockSpec((None, tq, ATT_WIDTH), lambda bi, i: (bi, i, 0)),
        out_shape=jax.ShapeDtypeStruct((b, s, ATT_WIDTH), BF16),
        scratch_shapes=[pltpu.VMEM((s, LANES), BF16)] + _flash_scratch(tq, s),
        compiler_params=_compiler_params(2),
        name="fox",
    )(p3, p3, p3, p3, cs, cst, pm)


def _moba_constants(s, bs):
    slopes2 = np.asarray(ALIBI_SLOPES, np.float32) * np.float32(LOG2E)
    pos = np.arange(s)
    r_s = (pos % bs).astype(np.float32)[:, None]
    block_onehot = (pos[:, None] // bs == np.arange(SUBLANES)[None, :]).astype(np.float32)
    augk = np.concatenate([r_s, r_s, r_s, np.ones((s, 3), np.float32), np.zeros((s, 2), np.float32),
                           block_onehot, block_onehot, block_onehot,
                           np.zeros((s, LANES - 4 * SUBLANES), np.float32)], axis=1)
    r_t = np.arange(bs, dtype=np.float32)
    augq = []
    for h in range(ATT_HEADS):
        m_pieces = [np.full((bs,), p, np.float32) for p in _split3_host(slopes2[h])]
        augq.append(np.stack(m_pieces + list(_split3_host(-slopes2[h] * r_t))
                             + [np.zeros((bs,), np.float32)] * 2, axis=0))
    return jnp.asarray(augk, BF16), jnp.asarray(np.stack(augq), F32)


def _moba_kernel(q_ref, k_ref, v_ref, z_ref, augk_ref, augq_ref, o_ref, kmt_ref, qt_ref, vt_ref):
    i = pl.program_id(1)
    bs = MOBA_BLOCK
    nb = k_ref.shape[0] // bs

    @pl.when(i == 0)
    def _():
        _transpose_values(v_ref, vt_ref, bs)
        kms = [jnp.mean(k_ref[j * bs:(j + 1) * bs, :].astype(F32), axis=0, keepdims=True) for j in range(nb)]
        kms += [jnp.zeros((1, ATT_WIDTH), F32)] * (SUBLANES - nb)
        km_all = jnp.concatenate(kms, axis=0)
        head_of_lane = lax.broadcasted_iota(jnp.int32, (SUBLANES, ATT_WIDTH), 1) // HEAD_DIM
        for h in range(ATT_HEADS):
            kmt_ref[h * SUBLANES:(h + 1) * SUBLANES, :] = jnp.where(head_of_lane == h, km_all, 0.0)

    q_t = [q_ref[:, hp * LANES:(hp + 1) * LANES].astype(F32).T for hp in range(ATT_HEADS // 2)]
    gate_t = _dot(kmt_ref[...].astype(BF16), jnp.concatenate(q_t, axis=0).astype(BF16))
    blk = lax.broadcasted_iota(jnp.int32, (SUBLANES, bs), 0)
    valid = blk < i
    back = (i - blk).astype(F32)
    for h in range(ATT_HEADS):
        gate = jnp.where(valid, gate_t[h * SUBLANES:(h + 1) * SUBLANES, :], NEG)
        rank = jnp.zeros((SUBLANES, bs), F32)
        for r in range(1, SUBLANES):
            other = pltpu.roll(gate, r, axis=0)
            lower = blk >= r
            rank = rank + jnp.where(other > gate, 1.0, 0.0) + jnp.where((other == gate) & lower, 1.0, 0.0)
        keep = (valid & (rank < MOBA_TOPK)) | (blk == i)
        bias = jnp.where(keep, -(ALIBI_SLOPES[h] * LOG2E * bs) * back, NEG)
        aug_rows = jnp.concatenate([augq_ref[h]] + list(_split3(bias)), axis=0)
        _store_qt(qt_ref, h, q_t[h // 2], aug_rows)

    _flash_all_heads(i, bs, k_ref, augk_ref, z_ref, o_ref, qt_ref, vt_ref)


def _moba(p3):
    b, s, _ = p3.shape
    bs = MOBA_BLOCK
    assert s % bs == 0 and s // bs <= SUBLANES
    augk, augq = _moba_constants(s, bs)
    tile = lambda c: pl.BlockSpec((None, bs, ATT_WIDTH), lambda bi, i, c=c: (bi, i, c))
    full = lambda c: pl.BlockSpec((None, s, ATT_WIDTH), lambda bi, i, c=c: (bi, 0, c))
    return pl.pallas_call(
        _moba_kernel,
        grid=(b, s // bs),
        in_specs=[tile(COL_B + 0), full(COL_B + 1), full(COL_B + 2), tile(COL_B + 3),
                  pl.BlockSpec(augk.shape, lambda bi, i: (0, 0)),
                  pl.BlockSpec(augq.shape, lambda bi, i: (0, 0, 0))],
        out_specs=pl.BlockSpec((None, bs, ATT_WIDTH), lambda bi, i: (bi, i, 0)),
        out_shape=jax.ShapeDtypeStruct((b, s, ATT_WIDTH), BF16),
        scratch_shapes=[pltpu.VMEM((ATT_HEADS * SUBLANES, ATT_WIDTH), F32)] + _flash_scratch(bs, s),
        compiler_params=_compiler_params(2),
        name="moba",
    )(p3, p3, p3, p3, augk, augq)


def _mlstm_kernel(qk_ref, v_ref, og_ref, z_ref, w_ref, cc_ref, cr_ref, rr_ref, hg_ref, y_ref,
                  cbuf_ref, cst_ref, mst_ref, fprev_ref, *, L):
    c = pl.program_id(1)
    halo = SUBLANES

    @pl.when(c == 0)
    def _():
        cbuf_ref[0:halo, :] = jnp.zeros((halo, cbuf_ref.shape[1]), F32)
        cst_ref[...] = jnp.zeros_like(cst_ref)
        mst_ref[...] = jnp.zeros_like(mst_ref)
        fprev_ref[...] = jnp.zeros_like(fprev_ref)

    @pl.when(c > 0)
    def _():
        cbuf_ref[0:halo, :] = cbuf_ref[L:L + halo, :]

    cbuf_ref[halo:halo + L, :] = qk_ref[...].astype(F32)
    conv = None
    for j in range(ML_CONV):
        off = halo - (ML_CONV - 1) + j
        term = cbuf_ref[off:off + L, :] * w_ref[j:j + 1, :]
        conv = term if conv is None else conv + term
    qk = conv * _sigmoid(conv)
    qk_w = ML_HEADS * HEAD_DIM
    q_b = qk[:, :qk_w].astype(BF16)
    k_s = qk[:, qk_w:] * ATT_SCALE
    k_b = k_s.astype(BF16)
    k_t = k_s.T

    row = lax.broadcasted_iota(jnp.int32, (L, L), 0)
    col = lax.broadcasted_iota(jnp.int32, (L, L), 1)
    tri = row >= col
    lane1 = lax.broadcasted_iota(jnp.int32, (1, LANES), 1)
    ones_col = jnp.broadcast_to(jnp.where(lane1 == 0, 1.0, 0.0), (L, LANES)).astype(BF16)

    for h in range(ML_HEADS):
        hp, hh = divmod(h, 2)
        lanes = slice(hp * LANES, (hp + 1) * LANES)
        qh = jnp.where(_head_lane_mask(hh), q_b[:, lanes], jnp.zeros((L, LANES), BF16))
        f_c = cc_ref[:, LANE_ML_F + h:LANE_ML_F + h + 1]
        g_r = cr_ref[LANE_ML_F + h:LANE_ML_F + h + 1, :] - rr_ref[LANE_ML_I + h:LANE_ML_I + h + 1, :]
        f0 = fprev_ref[0:1, LANE_ML_F + h:LANE_ML_F + h + 1]
        m_prev = mst_ref[h, 0:1, 0:1]

        d_intra = jnp.where(tri, f_c - g_r, NEG)
        d_inter = f_c - f0 + m_prev
        m_t = jnp.maximum(d_inter, jnp.max(d_intra, axis=1, keepdims=True))
        w_intra = jnp.exp(d_intra - m_t)
        w_inter = jnp.exp(d_inter - m_t)
        s_mat = (_dot_nt(qh, k_b[:, lanes]) * w_intra).astype(BF16)
        v_aug = jnp.concatenate([v_ref[:, h * ML_V_DIM:(h + 1) * ML_V_DIM], ones_col], axis=1)
        c_prev = cst_ref[hp]
        num = _dot(s_mat, v_aug) + w_inter * _dot(qh, c_prev.astype(BF16))
        den = num[:, ML_V_DIM:ML_V_DIM + 1]
        hv = num[:, :ML_V_DIM] / jnp.maximum(jnp.abs(den), jnp.exp(-m_t))

        f_last = f_c[L - 1:L, :]
        d_state = f_last - g_r
        m_new = jnp.maximum(f_last - f0 + m_prev, jnp.max(d_state, axis=1, keepdims=True))
        w_prev = jnp.exp(f_last - f0 + m_prev - m_new)
        w_s = jnp.exp(d_state - m_new)
        rows = slice(hh * HEAD_DIM, (hh + 1) * HEAD_DIM)
        kw = (k_t[h * HEAD_DIM:(h + 1) * HEAD_DIM, :] * w_s).astype(BF16)
        cst_ref[hp, rows, :] = w_prev * c_prev[rows, :] + _dot(kw, v_aug)
        mst_ref[h] = jnp.broadcast_to(m_new, mst_ref.shape[1:])

        vl = slice(h * ML_V_DIM, (h + 1) * ML_V_DIM)
        ho = hv * _sigmoid(og_ref[:, vl].astype(F32))
        yn = ho * lax.rsqrt(jnp.mean(ho * ho, axis=1, keepdims=True) + RMS_EPS) * hg_ref[:, vl]
        z = z_ref[:, vl].astype(F32)
        y_ref[:, vl] = (yn * (z * _sigmoid(z))).astype(BF16)

    fprev_ref[...] = cc_ref[L - 1:L, :]


def _mlstm(p3, conv_w, cs, cst, rawt, head_g, *, L=256):
    b, s, _ = p3.shape
    w512 = ML_HEADS * ML_V_DIM
    tile = lambda c: pl.BlockSpec((None, L, w512), lambda bi, i, c=c: (bi, i, c))
    rowt = pl.BlockSpec((None, GATE_ROWS, L), lambda bi, i: (bi, 0, i))
    return pl.pallas_call(
        functools.partial(_mlstm_kernel, L=L),
        grid=(b, s // L),
        in_specs=[tile(COL_C + 0), tile(COL_C + 1), tile(COL_C + 2), tile(COL_C + 3),
                  pl.BlockSpec((ML_CONV, w512), lambda bi, i: (0, 0)),
                  pl.BlockSpec((None, L, LANES), lambda bi, i: (bi, i, 0)),
                  rowt, rowt,
                  pl.BlockSpec((1, w512), lambda bi, i: (0, 0))],
        out_specs=pl.BlockSpec((None, L, w512), lambda bi, i: (bi, i, 0)),
        out_shape=jax.ShapeDtypeStruct((b, s, w512), BF16),
        scratch_shapes=[pltpu.VMEM((L + SUBLANES, w512), F32),
                        pltpu.VMEM((ML_HEADS // 2, 2 * HEAD_DIM, 2 * ML_V_DIM), F32),
                        pltpu.VMEM((ML_HEADS, SUBLANES, LANES), F32),
                        pltpu.VMEM((1, LANES), F32)],
        compiler_params=_compiler_params(2),
        name="mlstm",
    )(p3, p3, p3, p3, conv_w, cs, cst, rawt, head_g)


def _merge_kernel(ya_ref, yb_ref, yc_ref, g_ref, x_ref, wb_ref, wo_ref, fg_ref, o_ref, *, final):
    d = x_ref.shape[1]
    merged = None
    for n, y_ref in enumerate((ya_ref, yb_ref, yc_ref)):
        gate = _sigmoid(g_ref[:, n * d:(n + 1) * d].astype(F32))
        term = gate * _dot(y_ref[...], wb_ref[n])
        merged = term if merged is None else merged + term
    out = x_ref[...] + _dot(merged.astype(BF16), wo_ref[...])
    if final:
        out = out * lax.rsqrt(jnp.mean(out * out, axis=-1, keepdims=True) + RMS_EPS) * fg_ref[...]
    o_ref[...] = out


def _merge(ya, yb, yc, p2, x2, wb, wo, fg, *, final, tm=512):
    t, d = x2.shape
    w = ya.shape[1]
    ytile = pl.BlockSpec((tm, w), lambda i: (i, 0))
    return pl.pallas_call(
        functools.partial(_merge_kernel, final=final),
        grid=(t // tm,),
        in_specs=[ytile, ytile, ytile,
                  pl.BlockSpec((tm, N_BRANCHES * d), lambda i: (i, GATE_COL0 // (N_BRANCHES * d))),
                  pl.BlockSpec((tm, d), lambda i: (i, 0)),
                  pl.BlockSpec((N_BRANCHES, w, d), lambda i: (0, 0, 0)),
                  pl.BlockSpec((d, d), lambda i: (0, 0)),
                  pl.BlockSpec((1, d), lambda i: (0, 0))],
        out_specs=pl.BlockSpec((tm, d), lambda i: (i, 0)),
        out_shape=jax.ShapeDtypeStruct((t, d), F32),
        compiler_params=_compiler_params(1),
        name="merge_final" if final else "merge",
    )(ya, yb, yc, p2, x2, wb, wo, fg)


def _regroup_in_weights(w):
    aw = 4 * ATT_WIDTH
    b0 = aw + ATT_HEADS
    c_i0 = b0 + aw + 4 * ML_HEADS * ML_V_DIM
    g0 = c_i0 + 2 * ML_HEADS
    qscale = ATT_SCALE * LOG2E
    col = jnp.arange(aw)
    att_scale = jnp.where(col < ATT_WIDTH, qscale, 1.0).astype(F32)
    wa = (w[:, :aw] * att_scale).astype(BF16)
    bc_scale = jnp.concatenate([att_scale, jnp.ones((c_i0 - b0 - aw,), F32)])
    wbc = (w[:, b0:c_i0] * bc_scale).astype(BF16)
    wg = w[:, g0:].astype(BF16)
    small = jnp.concatenate([w[:, aw:b0], w[:, c_i0:g0]], axis=1)
    small = jnp.pad(small, ((0, 0), (0, LANES - small.shape[1])))
    return wa, wbc, wg, small


def kernel(x, norm_g, w_in, fox_b_f, mlstm_conv_w, mlstm_b_i, mlstm_b_f, mlstm_head_g, w_branch, w_out,
           final_norm_g):
    b, s, d = x.shape
    depth = w_in.shape[0]
    x2 = x.reshape(b * s, d)
    fg = final_norm_g.reshape(1, d)
    for layer in range(depth):
        wa, wbc, wg, small = _regroup_in_weights(w_in[layer])
        assert wa.shape[1] + wbc.shape[1] + wg.shape[1] == MAIN_WIDTH
        bias = jnp.concatenate([fox_b_f[layer], mlstm_b_i[layer], mlstm_b_f[layer]])
        bias = jnp.pad(bias, (0, LANES - bias.shape[0])).reshape(1, LANES)

        p2, gs = _inproj(x2, norm_g[layer].reshape(1, d), wa, wbc, wg, small)
        p3 = p2.reshape(b, s, MAIN_WIDTH)
        cs, cst, rawt = _gates(gs.reshape(b, s, LANES), bias)
        ya = _fox(p3, cs, cst)
        yb = _moba(p3)
        yc = _mlstm(p3, mlstm_conv_w[layer], cs, cst, rawt, mlstm_head_g[layer].reshape(1, -1))
        w512 = ya.shape[-1]
        x2 = _merge(ya.reshape(b * s, w512), yb.reshape(b * s, w512), yc.reshape(b * s, w512), p2, x2,
                    w_branch[layer].astype(BF16), w_out[layer].astype(BF16), fg,
                    final=(layer == depth - 1))
    return x2.reshape(b, s, d)
```

```python
import functools

import numpy as np
import jax
import jax.numpy as jnp
from jax import lax
from jax.experimental import pallas as pl
from jax.experimental.pallas import tpu as pltpu

F32 = jnp.float32
BF16 = jnp.bfloat16

D_MODEL = 1024
HEAD_DIM = 64
ATT_HEADS = 8
ATT_WIDTH = ATT_HEADS * HEAD_DIM
MOBA_BLOCK = 256
MOBA_TOPK = 3
ML_HEADS = 4
ML_V_DIM = 128
ML_CONV = 4
N_BRANCHES = 3
RMS_EPS = 1e-6
NEG = -1e30
ATT_SCALE = HEAD_DIM ** -0.5
LOG2E = float(np.log2(np.e))
V_ROWS = 80

LANES = 128
SUBLANES = 8
VMEM_LIMIT_BYTES = 52 * 1024 * 1024

COL_A = 0
COL_B = 4
COL_C = 8
GATE_COL0 = 6144
MAIN_WIDTH = GATE_COL0 + N_BRANCHES * D_MODEL
LANE_FOX_F = 0
LANE_ML_I = 8
LANE_ML_F = 12
GATE_ROWS = 16

ALIBI_SLOPES = tuple(
    float(v) for v in 2.0 ** (-8.0 * (np.arange(ATT_HEADS, dtype=np.float32) + 1.0) / ATT_HEADS))


def _dot(a, b):
    return jnp.dot(a, b, preferred_element_type=F32)


def _dot_nt(a, b):
    return lax.dot_general(a, b, (((1,), (1,)), ((), ())), preferred_element_type=F32)


def _sigmoid(x):
    return 1.0 / (1.0 + jnp.exp(-x))


def _compiler_params(n_axes):
    return pltpu.CompilerParams(dimension_semantics=("arbitrary",) * n_axes,
                                vmem_limit_bytes=VMEM_LIMIT_BYTES)


def _inproj_kernel(x_ref, g_ref, wa_ref, wbc_ref, wg_ref, ws_ref, p_ref, gs_ref, hn_ref, *, n_a, n_bc, rc):
    j = pl.program_id(1)
    tm = x_ref.shape[0]

    @pl.when(j == 0)
    def _():
        ws = ws_ref[...]
        wsh = ws.astype(BF16)
        ws2 = jnp.concatenate([wsh, (ws - wsh.astype(F32)).astype(BF16)], axis=1)
        for r0 in range(0, tm, rc):
            rows = slice(r0, r0 + rc)
            x = x_ref[rows, :]
            y = x * lax.rsqrt(jnp.mean(x * x, axis=-1, keepdims=True) + RMS_EPS) * g_ref[...]
            hi = y.astype(BF16)
            hn_ref[rows, :] = hi
            gs2 = _dot(hi, ws2)
            gs_ref[rows, :] = gs2[:, :LANES] + gs2[:, LANES:]
            p_ref[rows, :] = _dot(hi, wa_ref[...]).astype(BF16)

    @pl.when((j > 0) & (j < n_a))
    def _():
        p_ref[...] = _dot(hn_ref[...], wa_ref[...]).astype(BF16)

    @pl.when((j >= n_a) & (j < n_a + n_bc))
    def _():
        p_ref[...] = _dot(hn_ref[...], wbc_ref[...]).astype(BF16)

    @pl.when(j >= n_a + n_bc)
    def _():
        p_ref[...] = _dot(hn_ref[...], wg_ref[...]).astype(BF16)


def _inproj(x2, g, wa, wbc, wg, ws, *, tm=2048, tn=1024, rc=256):
    t, d = x2.shape
    tm = min(tm, t)
    n_a, n_bc, n_g = wa.shape[1] // tn, wbc.shape[1] // tn, wg.shape[1] // tn
    n = (n_a + n_bc + n_g) * tn
    return pl.pallas_call(
        functools.partial(_inproj_kernel, n_a=n_a, n_bc=n_bc, rc=rc),
        grid=(t // tm, n_a + n_bc + n_g),
        in_specs=[pl.BlockSpec((tm, d), lambda i, j: (i, 0)),
                  pl.BlockSpec((1, d), lambda i, j: (0, 0)),
                  pl.BlockSpec((d, tn), lambda i, j: (0, jnp.minimum(j, n_a - 1))),
                  pl.BlockSpec((d, tn), lambda i, j: (0, jnp.clip(j - n_a, 0, n_bc - 1))),
                  pl.BlockSpec((d, tn), lambda i, j: (0, jnp.clip(j - n_a - n_bc, 0, n_g - 1))),
                  pl.BlockSpec((d, LANES), lambda i, j: (0, 0))],
        out_specs=[pl.BlockSpec((tm, tn), lambda i, j: (i, j)),
                   pl.BlockSpec((tm, LANES), lambda i, j: (i, 0))],
        out_shape=[jax.ShapeDtypeStruct((t, n), BF16),
                   jax.ShapeDtypeStruct((t, LANES), F32)],
        scratch_shapes=[pltpu.VMEM((tm, d), BF16)],
        compiler_params=_compiler_params(2),
        name="inproj",
    )(x2, g, wa, wbc, wg, ws)


def _gates_kernel(gs_ref, bias_ref, cs_ref, cst_ref, rawt_ref, *, cb):
    s = gs_ref.shape[0]
    lane = lax.broadcasted_iota(jnp.int32, (1, LANES), 1)
    is_log_i = (lane >= LANE_ML_I) & (lane < LANE_ML_F)
    r = lax.broadcasted_iota(jnp.int32, (cb, cb), 0)
    c = lax.broadcasted_iota(jnp.int32, (cb, cb), 1)
    tri = jnp.where(r >= c, 1.0, 0.0).astype(BF16)
    carry = jnp.zeros((1, LANES), F32)
    for blk in range(s // cb):
        rows = slice(blk * cb, (blk + 1) * cb)
        g = gs_ref[rows, :] + bias_ref[...]
        log_sig = jnp.minimum(g, 0.0) - jnp.log(1.0 + jnp.exp(-jnp.abs(g)))
        raw = jnp.where(is_log_i, g, log_sig)
        hi = raw.astype(BF16)
        r1 = raw - hi.astype(F32)
        mid = r1.astype(BF16)
        lo = (r1 - mid.astype(F32)).astype(BF16)
        cs = _dot(tri, hi) + _dot(tri, mid) + _dot(tri, lo) + carry
        carry = cs[cb - 1:cb, :]
        cs_ref[rows, :] = cs
        cst_ref[:, rows] = cs.T[:GATE_ROWS, :]
        rawt_ref[:, rows] = raw.T[:GATE_ROWS, :]


def _gates(gs3, bias, *, cb=256):
    b, s, _ = gs3.shape
    col = pl.BlockSpec((None, s, LANES), lambda i: (i, 0, 0))
    row = pl.BlockSpec((None, GATE_ROWS, s), lambda i: (i, 0, 0))
    return pl.pallas_call(
        functools.partial(_gates_kernel, cb=cb),
        grid=(b,),
        in_specs=[col, pl.BlockSpec((1, LANES), lambda i: (0, 0))],
        out_specs=[col, row, row],
        out_shape=[jax.ShapeDtypeStruct((b, s, LANES), F32),
                   jax.ShapeDtypeStruct((b, GATE_ROWS, s), F32),
                   jax.ShapeDtypeStruct((b, GATE_ROWS, s), F32)],
        compiler_params=_compiler_params(1),
        name="gates",
    )(gs3, bias)


def _split3(x):
    hi = x.astype(BF16).astype(F32)
    r = x - hi
    mid = r.astype(BF16).astype(F32)
    lo = (r - mid).astype(BF16).astype(F32)
    return hi, mid, lo


def _split3_host(x):
    x = np.asarray(x, np.float32)
    hi = x.astype(BF16).astype(np.float32)
    mid = (x - hi).astype(BF16).astype(np.float32)
    lo = (x - hi - mid).astype(BF16).astype(np.float32)
    return hi, mid, lo


def _head_lane_mask(hh):
    lane = lax.broadcasted_iota(jnp.int32, (1, LANES), 1)
    return (lane >= HEAD_DIM * hh) & (lane < HEAD_DIM * (hh + 1))


def _store_qt(qt_ref, h, q_pair_t, aug_rows):
    chan = lax.broadcasted_iota(jnp.int32, (LANES, 1), 0)
    hh = h % 2
    own = (chan >= HEAD_DIM * hh) & (chan < HEAD_DIM * (hh + 1))
    pad = jnp.zeros((LANES - aug_rows.shape[0], aug_rows.shape[1]), F32)
    qt_ref[h] = jnp.concatenate([jnp.where(own, q_pair_t, 0.0), aug_rows, pad], axis=0).astype(BF16)


def _transpose_values(v_ref, vt_ref, blk):
    s = v_ref.shape[0]
    for b0 in range(0, s, blk):
        for hp in range(ATT_HEADS // 2):
            vb = v_ref[b0:b0 + blk, hp * LANES:(hp + 1) * LANES].astype(F32).T.astype(BF16)
            for hh in range(2):
                vt_ref[2 * hp + hh, 0:HEAD_DIM, b0:b0 + blk] = vb[hh * HEAD_DIM:(hh + 1) * HEAD_DIM, :]
    extra = V_ROWS - HEAD_DIM
    ones_row = jnp.where(lax.broadcasted_iota(jnp.int32, (extra, s), 0) == 0, 1.0, 0.0).astype(BF16)
    for h in range(ATT_HEADS):
        vt_ref[h, HEAD_DIM:V_ROWS, :] = ones_row


def _flash_all_heads(i, tq, k_ref, aug_ref, z_ref, o_ref, qt_ref, vt_ref):
    key = lax.broadcasted_iota(jnp.int32, (tq, tq), 0)
    qry = lax.broadcasted_iota(jnp.int32, (tq, tq), 1)
    causal = key <= qry

    def scores_of(start, diag):
        aug = aug_ref[pl.ds(start, tq), :]
        out = []
        for hp in range(ATT_HEADS // 2):
            kj = jnp.concatenate([k_ref[pl.ds(start, tq), hp * LANES:(hp + 1) * LANES], aug], axis=1)
            for h in (2 * hp, 2 * hp + 1):
                s = _dot(kj, qt_ref[h])
                out.append(jnp.where(causal, s, NEG) if diag else s)
        return out

    def absorb(start, scores, state):
        new_state = []
        for h in range(ATT_HEADS):
            vt = vt_ref[h, :, pl.ds(start, tq)]
            s = scores[h]
            if state is None:
                m = jnp.max(s, axis=0, keepdims=True)
                acc = _dot(vt, jnp.exp2(s - m).astype(BF16))
            else:
                m_old, acc_old = state[h]
                m = jnp.maximum(m_old, jnp.max(s, axis=0, keepdims=True))
                acc = jnp.exp2(m_old - m) * acc_old + _dot(vt, jnp.exp2(s - m).astype(BF16))
            new_state.append((m, acc))
        return tuple(new_state)

    def pair(jj, state):
        a = pl.multiple_of(2 * jj * tq, 2 * tq)
        b = pl.multiple_of(a + tq, tq)
        scores_a, scores_b = scores_of(a, False), scores_of(b, False)
        return absorb(b, scores_b, absorb(a, scores_a, state))

    def last_odd(_, state):
        start = pl.multiple_of((i - 1) * tq, tq)
        return absorb(start, scores_of(start, False), state)

    diag0 = pl.multiple_of(i * tq, tq)
    state = absorb(diag0, scores_of(diag0, True), None)
    state = lax.fori_loop(0, lax.shift_right_logical(i, 1), pair, state)
    state = lax.fori_loop(0, i & 1, last_odd, state)

    for hp in range(ATT_HEADS // 2):
        lanes = slice(hp * LANES, (hp + 1) * LANES)
        outs = [acc[:HEAD_DIM, :] / acc[HEAD_DIM:HEAD_DIM + 1, :] for _, acc in state[2 * hp:2 * hp + 2]]
        pair = jnp.concatenate(outs, axis=0).T
        z = z_ref[:, lanes].astype(F32)
        o_ref[:, lanes] = (pair * (z * _sigmoid(z))).astype(BF16)


def _flash_scratch(tq, s):
    return [pltpu.VMEM((ATT_HEADS, 2 * LANES, tq), BF16),
            pltpu.VMEM((ATT_HEADS, V_ROWS, s), BF16)]


FOX_ONES0 = ATT_HEADS * SUBLANES


def _fox_key_placement():
    pm = np.zeros((3 * LANES, LANES), np.float32)
    for h in range(ATT_HEADS):
        for p in range(3):
            pm[p * LANES + LANE_FOX_F + h, SUBLANES * h + p] = -1.0
    return jnp.asarray(pm, BF16)


def _fox_kernel(q_ref, k_ref, v_ref, z_ref, cc_ref, cr_ref, pm_ref, o_ref, aug_ref, qt_ref, vt_ref, *, tq):
    i = pl.program_id(1)
    lane = lax.broadcasted_iota(jnp.int32, (1, LANES), 1)

    @pl.when(i == 0)
    def _():
        _transpose_values(v_ref, vt_ref, tq)
        for b0 in range(0, k_ref.shape[0], tq):
            pieces = _split3(cc_ref[b0:b0 + tq, :] * LOG2E)
            aug = _dot(jnp.concatenate(pieces, axis=1).astype(BF16), pm_ref[...])
            aug_ref[b0:b0 + tq, :] = jnp.where((lane >= FOX_ONES0) & (lane < FOX_ONES0 + 3), 1.0, aug).astype(BF16)

    q0 = pl.multiple_of(i * tq, tq)
    c_hi, c_mid, c_lo = _split3(cr_ref[:, pl.ds(q0, tq)] * LOG2E)
    row = lax.broadcasted_iota(jnp.int32, (FOX_ONES0, tq), 0)
    for hp in range(ATT_HEADS // 2):
        qp = q_ref[:, hp * LANES:(hp + 1) * LANES].astype(F32).T
        for h in (2 * hp, 2 * hp + 1):
            r = LANE_FOX_F + h
            ones_rows = jnp.where((row >= SUBLANES * h) & (row < SUBLANES * h + 3), 1.0, 0.0)
            aug_rows = jnp.concatenate([ones_rows, c_hi[r:r + 1], c_mid[r:r + 1], c_lo[r:r + 1]], axis=0)
            _store_qt(qt_ref, h, qp, aug_rows)

    _flash_all_heads(i, tq, k_ref, aug_ref, z_ref, o_ref, qt_ref, vt_ref)


def _fox(p3, cs, cst, *, tq=256):
    b, s, _ = p3.shape
    tile = lambda c: pl.BlockSpec((None, tq, ATT_WIDTH), lambda bi, i, c=c: (bi, i, c))
    full = lambda c: pl.BlockSpec((None, s, ATT_WIDTH), lambda bi, i, c=c: (bi, 0, c))
    pm = _fox_key_placement()
    return pl.pallas_call(
        functools.partial(_fox_kernel, tq=tq),
        grid=(b, s // tq),
        in_specs=[tile(COL_A + 0), full(COL_A + 1), full(COL_A + 2), tile(COL_A + 3),
                  pl.BlockSpec((None, s, LANES), lambda bi, i: (bi, 0, 0)),
                  pl.BlockSpec((None, GATE_ROWS, s), lambda bi, i: (bi, 0, 0)),
                  pl.BlockSpec(pm.shape, lambda bi, i: (0, 0))],
        out_specs=pl.BlockSpec((None, tq, ATT_WIDTH), lambda bi, i: (bi, i, 0)),
        out_shape=jax.ShapeDtypeStruct((b, s, ATT_WIDTH), BF16),
        scratch_shapes=[pltpu.VMEM((s, LANES), BF16)] + _flash_scratch(tq, s),
        compiler_params=_compiler_params(2),
        name="fox",
    )(p3, p3, p3, p3, cs, cst, pm)


def _moba_constants(s, bs):
    slopes2 = np.asarray(ALIBI_SLOPES, np.float32) * np.float32(LOG2E)
    pos = np.arange(s)
    r_s = (pos % bs).astype(np.float32)[:, None]
    block_onehot = (pos[:, None] // bs == np.arange(SUBLANES)[None, :]).astype(np.float32)
    augk = np.concatenate([r_s, r_s, r_s, np.ones((s, 3), np.float32), np.zeros((s, 2), np.float32),
                           block_onehot, block_onehot, block_onehot,
                           np.zeros((s, LANES - 4 * SUBLANES), np.float32)], axis=1)
    r_t = np.arange(bs, dtype=np.float32)
    augq = []
    for h in range(ATT_HEADS):
        m_pieces = [np.full((bs,), p, np.float32) for p in _split3_host(slopes2[h])]
        augq.append(np.stack(m_pieces + list(_split3_host(-slopes2[h] * r_t))
                             + [np.zeros((bs,), np.float32)] * 2, axis=0))
    return jnp.asarray(augk, BF16), jnp.asarray(np.stack(augq), F32)


def _moba_kernel(q_ref, k_ref, v_ref, z_ref, augk_ref, augq_ref, o_ref, kmt_ref, qt_ref, vt_ref):
    i = pl.program_id(1)
    bs = MOBA_BLOCK
    nb = k_ref.shape[0] // bs

    @pl.when(i == 0)
    def _():
        _transpose_values(v_ref, vt_ref, bs)
        kms = [jnp.mean(k_ref[j * bs:(j + 1) * bs, :].astype(F32), axis=0, keepdims=True) for j in range(nb)]
        kms += [jnp.zeros((1, ATT_WIDTH), F32)] * (SUBLANES - nb)
        km_all = jnp.concatenate(kms, axis=0)
        head_of_lane = lax.broadcasted_iota(jnp.int32, (SUBLANES, ATT_WIDTH), 1) // HEAD_DIM
        for h in range(ATT_HEADS):
            kmt_ref[h * SUBLANES:(h + 1) * SUBLANES, :] = jnp.where(head_of_lane == h, km_all, 0.0)

    q_t = [q_ref[:, hp * LANES:(hp + 1) * LANES].astype(F32).T for hp in range(ATT_HEADS // 2)]
    gate_t = _dot(kmt_ref[...].astype(BF16), jnp.concatenate(q_t, axis=0).astype(BF16))
    blk = lax.broadcasted_iota(jnp.int32, (SUBLANES, bs), 0)
    valid = blk < i
    back = (i - blk).astype(F32)
    for h in range(ATT_HEADS):
        gate = jnp.where(valid, gate_t[h * SUBLANES:(h + 1) * SUBLANES, :], NEG)
        rank = jnp.zeros((SUBLANES, bs), F32)
        for r in range(1, SUBLANES):
            other = pltpu.roll(gate, r, axis=0)
            lower = blk >= r
            rank = rank + jnp.where(other > gate, 1.0, 0.0) + jnp.where((other == gate) & lower, 1.0, 0.0)
        keep = (valid & (rank < MOBA_TOPK)) | (blk == i)
        bias = jnp.where(keep, -(ALIBI_SLOPES[h] * LOG2E * bs) * back, NEG)
        aug_rows = jnp.concatenate([augq_ref[h]] + list(_split3(bias)), axis=0)
        _store_qt(qt_ref, h, q_t[h // 2], aug_rows)

    _flash_all_heads(i, bs, k_ref, augk_ref, z_ref, o_ref, qt_ref, vt_ref)


def _moba(p3):
    b, s, _ = p3.shape
    bs = MOBA_BLOCK
    assert s % bs == 0 and s // bs <= SUBLANES
    augk, augq = _moba_constants(s, bs)
    tile = lambda c: pl.BlockSpec((None, bs, ATT_WIDTH), lambda bi, i, c=c: (bi, i, c))
    full = lambda c: pl.BlockSpec((None, s, ATT_WIDTH), lambda bi, i, c=c: (bi, 0, c))
    return pl.pallas_call(
        _moba_kernel,
        grid=(b, s // bs),
        in_specs=[tile(COL_B + 0), full(COL_B + 1), full(COL_B + 2), tile(COL_B + 3),
                  pl.BlockSpec(augk.shape, lambda bi, i: (0, 0)),
                  pl.BlockSpec(augq.shape, lambda bi, i: (0, 0, 0))],
        out_specs=pl.BlockSpec((None, bs, ATT_WIDTH), lambda bi, i: (bi, i, 0)),
        out_shape=jax.ShapeDtypeStruct((b, s, ATT_WIDTH), BF16),
        scratch_shapes=[pltpu.VMEM((ATT_HEADS * SUBLANES, ATT_WIDTH), F32)] + _flash_scratch(bs, s),
        compiler_params=_compiler_params(2),
        name="moba",
    )(p3, p3, p3, p3, augk, augq)


def _mlstm_kernel(qk_ref, v_ref, og_ref, z_ref, w_ref, cc_ref, cr_ref, rr_ref, hg_ref, y_ref,
                  cbuf_ref, cst_ref, mst_ref, fprev_ref, *, L):
    c = pl.program_id(1)
    halo = SUBLANES

    @pl.when(c == 0)
    def _():
        cbuf_ref[0:halo, :] = jnp.zeros((halo, cbuf_ref.shape[1]), F32)
        cst_ref[...] = jnp.zeros_like(cst_ref)
        mst_ref[...] = jnp.zeros_like(mst_ref)
        fprev_ref[...] = jnp.zeros_like(fprev_ref)

    @pl.when(c > 0)
    def _():
        cbuf_ref[0:halo, :] = cbuf_ref[L:L + halo, :]

    cbuf_ref[halo:halo + L, :] = qk_ref[...].astype(F32)
    conv = None
    for j in range(ML_CONV):
        off = halo - (ML_CONV - 1) + j
        term = cbuf_ref[off:off + L, :] * w_ref[j:j + 1, :]
        conv = term if conv is None else conv + term
    qk = conv * _sigmoid(conv)
    qk_w = ML_HEADS * HEAD_DIM
    q_b = qk[:, :qk_w].astype(BF16)
    k_s = qk[:, qk_w:] * ATT_SCALE
    k_b = k_s.astype(BF16)
    k_t = k_s.T

    row = lax.broadcasted_iota(jnp.int32, (L, L), 0)
    col = lax.broadcasted_iota(jnp.int32, (L, L), 1)
    tri = row >= col
    lane1 = lax.broadcasted_iota(jnp.int32, (1, LANES), 1)
    ones_col = jnp.broadcast_to(jnp.where(lane1 == 0, 1.0, 0.0), (L, LANES)).astype(BF16)

    for h in range(ML_HEADS):
        hp, hh = divmod(h, 2)
        lanes = slice(hp * LANES, (hp + 1) * LANES)
        qh = jnp.where(_head_lane_mask(hh), q_b[:, lanes], jnp.zeros((L, LANES), BF16))
        f_c = cc_ref[:, LANE_ML_F + h:LANE_ML_F + h + 1]
        g_r = cr_ref[LANE_ML_F + h:LANE_ML_F + h + 1, :] - rr_ref[LANE_ML_I + h:LANE_ML_I + h + 1, :]
        f0 = fprev_ref[0:1, LANE_ML_F + h:LANE_ML_F + h + 1]
        m_prev = mst_ref[h, 0:1, 0:1]

        d_intra = jnp.where(tri, f_c - g_r, NEG)
        d_inter = f_c - f0 + m_prev
        m_t = jnp.maximum(d_inter, jnp.max(d_intra, axis=1, keepdims=True))
        w_intra = jnp.exp(d_intra - m_t)
        w_inter = jnp.exp(d_inter - m_t)
        s_mat = (_dot_nt(qh, k_b[:, lanes]) * w_intra).astype(BF16)
        v_aug = jnp.concatenate([v_ref[:, h * ML_V_DIM:(h + 1) * ML_V_DIM], ones_col], axis=1)
        c_prev = cst_ref[hp]
        num = _dot(s_mat, v_aug) + w_inter * _dot(qh, c_prev.astype(BF16))
        den = num[:, ML_V_DIM:ML_V_DIM + 1]
        hv = num[:, :ML_V_DIM] / jnp.maximum(jnp.abs(den), jnp.exp(-m_t))

        f_last = f_c[L - 1:L, :]
        d_state = f_last - g_r
        m_new = jnp.maximum(f_last - f0 + m_prev, jnp.max(d_state, axis=1, keepdims=True))
        w_prev = jnp.exp(f_last - f0 + m_prev - m_new)
        w_s = jnp.exp(d_state - m_new)
        rows = slice(hh * HEAD_DIM, (hh + 1) * HEAD_DIM)
        kw = (k_t[h * HEAD_DIM:(h + 1) * HEAD_DIM, :] * w_s).astype(BF16)
        cst_ref[hp, rows, :] = w_prev * c_prev[rows, :] + _dot(kw, v_aug)
        mst_ref[h] = jnp.broadcast_to(m_new, mst_ref.shape[1:])

        vl = slice(h * ML_V_DIM, (h + 1) * ML_V_DIM)
        ho = hv * _sigmoid(og_ref[:, vl].astype(F32))
        yn = ho * lax.rsqrt(jnp.mean(ho * ho, axis=1, keepdims=True) + RMS_EPS) * hg_ref[:, vl]
        z = z_ref[:, vl].astype(F32)
        y_ref[:, vl] = (yn * (z * _sigmoid(z))).astype(BF16)

    fprev_ref[...] = cc_ref[L - 1:L, :]


def _mlstm(p3, conv_w, cs, cst, rawt, head_g, *, L=256):
    b, s, _ = p3.shape
    w512 = ML_HEADS * ML_V_DIM
    tile = lambda c: pl.BlockSpec((None, L, w512), lambda bi, i, c=c: (bi, i, c))
    rowt = pl.BlockSpec((None, GATE_ROWS, L), lambda bi, i: (bi, 0, i))
    return pl.pallas_call(
        functools.partial(_mlstm_kernel, L=L),
        grid=(b, s // L),
        in_specs=[tile(COL_C + 0), tile(COL_C + 1), tile(COL_C + 2), tile(COL_C + 3),
                  pl.BlockSpec((ML_CONV, w512), lambda bi, i: (0, 0)),
                  pl.BlockSpec((None, L, LANES), lambda bi, i: (bi, i, 0)),
                  rowt, rowt,
                  pl.BlockSpec((1, w512), lambda bi, i: (0, 0))],
        out_specs=pl.BlockSpec((None, L, w512), lambda bi, i: (bi, i, 0)),
        out_shape=jax.ShapeDtypeStruct((b, s, w512), BF16),
        scratch_shapes=[pltpu.VMEM((L + SUBLANES, w512), F32),
                        pltpu.VMEM((ML_HEADS // 2, 2 * HEAD_DIM, 2 * ML_V_DIM), F32),
                        pltpu.VMEM((ML_HEADS, SUBLANES, LANES), F32),
                        pltpu.VMEM((1, LANES), F32)],
        compiler_params=_compiler_params(2),
        name="mlstm",
    )(p3, p3, p3, p3, conv_w, cs, cst, rawt, head_g)


def _merge_kernel(ya_ref, yb_ref, yc_ref, g_ref, x_ref, wb_ref, wo_ref, fg_ref, o_ref, *, final):
    d = x_ref.shape[1]
    merged = None
    for n, y_ref in enumerate((ya_ref, yb_ref, yc_ref)):
        gate = _sigmoid(g_ref[:, n * d:(n + 1) * d].astype(F32))
        term = gate * _dot(y_ref[...], wb_ref[n])
        merged = term if merged is None else merged + term
    out = x_ref[...] + _dot(merged.astype(BF16), wo_ref[...])
    if final:
        out = out * lax.rsqrt(jnp.mean(out * out, axis=-1, keepdims=True) + RMS_EPS) * fg_ref[...]
    o_ref[...] = out


def _merge(ya, yb, yc, p2, x2, wb, wo, fg, *, final, tm=512):
    t, d = x2.shape
    w = ya.shape[1]
    ytile = pl.BlockSpec((tm, w), lambda i: (i, 0))
    return pl.pallas_call(
        functools.partial(_merge_kernel, final=final),
        grid=(t // tm,),
        in_specs=[ytile, ytile, ytile,
                  pl.BlockSpec((tm, N_BRANCHES * d), lambda i: (i, GATE_COL0 // (N_BRANCHES * d))),
                  pl.BlockSpec((tm, d), lambda i: (i, 0)),
                  pl.BlockSpec((N_BRANCHES, w, d), lambda i: (0, 0, 0)),
                  pl.BlockSpec((d, d), lambda i: (0, 0)),
                  pl.BlockSpec((1, d), lambda i: (0, 0))],
        out_specs=pl.BlockSpec((tm, d), lambda i: (i, 0)),
        out_shape=jax.ShapeDtypeStruct((t, d), F32),
        compiler_params=_compiler_params(1),
        name="merge_final" if final else "merge",
    )(ya, yb, yc, p2, x2, wb, wo, fg)


def _regroup_in_weights(w):
    aw = 4 * ATT_WIDTH
    b0 = aw + ATT_HEADS
    c_i0 = b0 + aw + 4 * ML_HEADS * ML_V_DIM
    g0 = c_i0 + 2 * ML_HEADS
    qscale = ATT_SCALE * LOG2E
    col = jnp.arange(aw)
    att_scale = jnp.where(col < ATT_WIDTH, qscale, 1.0).astype(F32)
    wa = (w[:, :aw] * att_scale).astype(BF16)
    bc_scale = jnp.concatenate([att_scale, jnp.ones((c_i0 - b0 - aw,), F32)])
    wbc = (w[:, b0:c_i0] * bc_scale).astype(BF16)
    wg = w[:, g0:].astype(BF16)
    small = jnp.concatenate([w[:, aw:b0], w[:, c_i0:g0]], axis=1)
    small = jnp.pad(small, ((0, 0), (0, LANES - small.shape[1])))
    return wa, wbc, wg, small


def kernel(x, norm_g, w_in, fox_b_f, mlstm_conv_w, mlstm_b_i, mlstm_b_f, mlstm_head_g, w_branch, w_out,
           final_norm_g):
    b, s, d = x.shape
    depth = w_in.shape[0]
    x2 = x.reshape(b * s, d)
    fg = final_norm_g.reshape(1, d)
    for layer in range(depth):
        wa, wbc, wg, small = _regroup_in_weights(w_in[layer])
        assert wa.shape[1] + wbc.shape[1] + wg.shape[1] == MAIN_WIDTH
        bias = jnp.concatenate([fox_b_f[layer], mlstm_b_i[layer], mlstm_b_f[layer]])
        bias = jnp.pad(bias, (0, LANES - bias.shape[0])).reshape(1, LANES)

        p2, gs = _inproj(x2, norm_g[layer].reshape(1, d), wa, wbc, wg, small)
        p3 = p2.reshape(b, s, MAIN_WIDTH)
        cs, cst, rawt = _gates(gs.reshape(b, s, LANES), bias)
        ya = _fox(p3, cs, cst)
        yb = _moba(p3)
        yc = _mlstm(p3, mlstm_conv_w[layer], cs, cst, rawt, mlstm_head_g[layer].reshape(1, -1))
        w512 = ya.shape[-1]
        x2 = _merge(ya.reshape(b * s, w512), yb.reshape(b * s, w512), yc.reshape(b * s, w512), p2, x2,
                    w_branch[layer].astype(BF16), w_out[layer].astype(BF16), fg,
                    final=(layer == depth - 1))
    return x2.reshape(b, s, d)
```

```python
import functools

import numpy as np
import jax
import jax.numpy as jnp
from jax import lax
from jax.experimental import pallas as pl
from jax.experimental.pallas import tpu as pltpu

F32 = jnp.float32
BF16 = jnp.bfloat16

D_MODEL = 1024
HEAD_DIM = 64
ATT_HEADS = 8
ATT_WIDTH = ATT_HEADS * HEAD_DIM
MOBA_BLOCK = 256
MOBA_TOPK = 3
ML_HEADS = 4
ML_V_DIM = 128
ML_CONV = 4
ML_ROWS = 2
ML_CHAINS = 4
N_BRANCHES = 3
RMS_EPS = 1e-6
NEG = -1e30
ATT_SCALE = HEAD_DIM ** -0.5
LOG2E = float(np.log2(np.e))
V_ROWS = 80

LANES = 128
SUBLANES = 8
VMEM_LIMIT_BYTES = 52 * 1024 * 1024

COL_A = 0
COL_B = 4
COL_C = 8
GATE_COL0 = 6144
MAIN_WIDTH = GATE_COL0 + N_BRANCHES * D_MODEL
LANE_FOX_F = 0
LANE_ML_I = 8
LANE_ML_F = 12
GATE_ROWS = 16

ALIBI_SLOPES = tuple(
    float(v) for v in 2.0 ** (-8.0 * (np.arange(ATT_HEADS, dtype=np.float32) + 1.0) / ATT_HEADS))


def _dot(a, b):
    return jnp.dot(a, b, preferred_element_type=F32)


def _dot_nt(a, b):
    return lax.dot_general(a, b, (((1,), (1,)), ((), ())), preferred_element_type=F32)


def _sigmoid(x):
    return 1.0 / (1.0 + jnp.exp(-x))


def _compiler_params(n_axes):
    return pltpu.CompilerParams(dimension_semantics=("arbitrary",) * n_axes,
                                vmem_limit_bytes=VMEM_LIMIT_BYTES)


def _inproj_kernel(x_ref, g_ref, wa_ref, wbc_ref, wg_ref, ws_ref, p_ref, gs_ref, hn_ref, *, n_a, n_bc, rc):
    j = pl.program_id(1)
    tm = x_ref.shape[0]

    @pl.when(j == 0)
    def _():
        ws = ws_ref[...]
        wsh = ws.astype(BF16)
        ws2 = jnp.concatenate([wsh, (ws - wsh.astype(F32)).astype(BF16)], axis=1)
        for r0 in range(0, tm, rc):
            rows = slice(r0, r0 + rc)
            x = x_ref[rows, :]
            y = x * lax.rsqrt(jnp.mean(x * x, axis=-1, keepdims=True) + RMS_EPS) * g_ref[...]
            hi = y.astype(BF16)
            hn_ref[rows, :] = hi
            gs2 = _dot(hi, ws2)
            gs_ref[rows, :] = gs2[:, :LANES] + gs2[:, LANES:]
            p_ref[rows, :] = _dot(hi, wa_ref[...]).astype(BF16)

    @pl.when((j > 0) & (j < n_a))
    def _():
        p_ref[...] = _dot(hn_ref[...], wa_ref[...]).astype(BF16)

    @pl.when((j >= n_a) & (j < n_a + n_bc))
    def _():
        p_ref[...] = _dot(hn_ref[...], wbc_ref[...]).astype(BF16)

    @pl.when(j >= n_a + n_bc)
    def _():
        p_ref[...] = _dot(hn_ref[...], wg_ref[...]).astype(BF16)


def _inproj(x2, g, wa, wbc, wg, ws, *, tm=2048, tn=1024, rc=256):
    t, d = x2.shape
    tm = min(tm, t)
    n_a, n_bc, n_g = wa.shape[1] // tn, wbc.shape[1] // tn, wg.shape[1] // tn
    n = (n_a + n_bc + n_g) * tn
    return pl.pallas_call(
        functools.partial(_inproj_kernel, n_a=n_a, n_bc=n_bc, rc=rc),
        grid=(t // tm, n_a + n_bc + n_g),
        in_specs=[pl.BlockSpec((tm, d), lambda i, j: (i, 0)),
                  pl.BlockSpec((1, d), lambda i, j: (0, 0)),
                  pl.BlockSpec((d, tn), lambda i, j: (0, jnp.minimum(j, n_a - 1))),
                  pl.BlockSpec((d, tn), lambda i, j: (0, jnp.clip(j - n_a, 0, n_bc - 1))),
                  pl.BlockSpec((d, tn), lambda i, j: (0, jnp.clip(j - n_a - n_bc, 0, n_g - 1))),
                  pl.BlockSpec((d, LANES), lambda i, j: (0, 0))],
        out_specs=[pl.BlockSpec((tm, tn), lambda i, j: (i, j)),
                   pl.BlockSpec((tm, LANES), lambda i, j: (i, 0))],
        out_shape=[jax.ShapeDtypeStruct((t, n), BF16),
                   jax.ShapeDtypeStruct((t, LANES), F32)],
        scratch_shapes=[pltpu.VMEM((tm, d), BF16)],
        compiler_params=_compiler_params(2),
        name="inproj",
    )(x2, g, wa, wbc, wg, ws)


def _gates_kernel(gs_ref, bias_ref, cs_ref, cst_ref, rawt_ref, *, cb):
    s = gs_ref.shape[0]
    lane = lax.broadcasted_iota(jnp.int32, (1, LANES), 1)
    is_log_i = (lane >= LANE_ML_I) & (lane < LANE_ML_F)
    r = lax.broadcasted_iota(jnp.int32, (cb, cb), 0)
    c = lax.broadcasted_iota(jnp.int32, (cb, cb), 1)
    tri = jnp.where(r >= c, 1.0, 0.0).astype(BF16)
    carry = jnp.zeros((1, LANES), F32)
    for blk in range(s // cb):
        rows = slice(blk * cb, (blk + 1) * cb)
        g = gs_ref[rows, :] + bias_ref[...]
        log_sig = jnp.minimum(g, 0.0) - jnp.log(1.0 + jnp.exp(-jnp.abs(g)))
        raw = jnp.where(is_log_i, g, log_sig)
        hi = raw.astype(BF16)
        r1 = raw - hi.astype(F32)
        mid = r1.astype(BF16)
        lo = (r1 - mid.astype(F32)).astype(BF16)
        cs = _dot(tri, hi) + _dot(tri, mid) + _dot(tri, lo) + carry
        carry = cs[cb - 1:cb, :]
        cs_ref[rows, :] = cs
        cst_ref[:, rows] = cs.T[:GATE_ROWS, :]
        rawt_ref[:, rows] = raw.T[:GATE_ROWS, :]


def _gates(gs3, bias, *, cb=256):
    b, s, _ = gs3.shape
    col = pl.BlockSpec((None, s, LANES), lambda i: (i, 0, 0))
    row = pl.BlockSpec((None, GATE_ROWS, s), lambda i: (i, 0, 0))
    return pl.pallas_call(
        functools.partial(_gates_kernel, cb=cb),
        grid=(b,),
        in_specs=[col, pl.BlockSpec((1, LANES), lambda i: (0, 0))],
        out_specs=[col, row, row],
        out_shape=[jax.ShapeDtypeStruct((b, s, LANES), F32),
                   jax.ShapeDtypeStruct((b, GATE_ROWS, s), F32),
                   jax.ShapeDtypeStruct((b, GATE_ROWS, s), F32)],
        compiler_params=_compiler_params(1),
        name="gates",
    )(gs3, bias)


def _split3(x):
    hi = x.astype(BF16).astype(F32)
    r = x - hi
    mid = r.astype(BF16).astype(F32)
    lo = (r - mid).astype(BF16).astype(F32)
    return hi, mid, lo


def _split3_host(x):
    x = np.asarray(x, np.float32)
    hi = x.astype(BF16).astype(np.float32)
    mid = (x - hi).astype(BF16).astype(np.float32)
    lo = (x - hi - mid).astype(BF16).astype(np.float32)
    return hi, mid, lo


def _head_lane_mask(hh):
    lane = lax.broadcasted_iota(jnp.int32, (1, LANES), 1)
    return (lane >= HEAD_DIM * hh) & (lane < HEAD_DIM * (hh + 1))


def _store_qt(qt_ref, h, q_pair_t, aug_rows):
    chan = lax.broadcasted_iota(jnp.int32, (LANES, 1), 0)
    hh = h % 2
    own = (chan >= HEAD_DIM * hh) & (chan < HEAD_DIM * (hh + 1))
    pad = jnp.zeros((LANES - aug_rows.shape[0], aug_rows.shape[1]), F32)
    qt_ref[h] = jnp.concatenate([jnp.where(own, q_pair_t, 0.0), aug_rows, pad], axis=0).astype(BF16)


def _transpose_values(v_ref, vt_ref, blk):
    s = v_ref.shape[0]
    for b0 in range(0, s, blk):
        for hp in range(ATT_HEADS // 2):
            vb = v_ref[b0:b0 + blk, hp * LANES:(hp + 1) * LANES].astype(F32).T.astype(BF16)
            for hh in range(2):
                vt_ref[2 * hp + hh, 0:HEAD_DIM, b0:b0 + blk] = vb[hh * HEAD_DIM:(hh + 1) * HEAD_DIM, :]
    extra = V_ROWS - HEAD_DIM
    ones_row = jnp.where(lax.broadcasted_iota(jnp.int32, (extra, s), 0) == 0, 1.0, 0.0).astype(BF16)
    for h in range(ATT_HEADS):
        vt_ref[h, HEAD_DIM:V_ROWS, :] = ones_row


def _flash_all_heads(i, tq, k_ref, aug_ref, z_ref, o_ref, qt_ref, vt_ref):
    key = lax.broadcasted_iota(jnp.int32, (tq, tq), 0)
    qry = lax.broadcasted_iota(jnp.int32, (tq, tq), 1)
    causal = key <= qry

    def scores_of(start, diag):
        aug = aug_ref[pl.ds(start, tq), :]
        out = []
        for hp in range(ATT_HEADS // 2):
            kj = jnp.concatenate([k_ref[pl.ds(start, tq), hp * LANES:(hp + 1) * LANES], aug], axis=1)
            for h in (2 * hp, 2 * hp + 1):
                s = _dot(kj, qt_ref[h])
                out.append(jnp.where(causal, s, NEG) if diag else s)
        return out

    def absorb(start, scores, state):
        new_state = []
        for h in range(ATT_HEADS):
            vt = vt_ref[h, :, pl.ds(start, tq)]
            s = scores[h]
            if state is None:
                m = jnp.max(s, axis=0, keepdims=True)
                acc = _dot(vt, jnp.exp2(s - m).astype(BF16))
            else:
                m_old, acc_old = state[h]
                m = jnp.maximum(m_old, jnp.max(s, axis=0, keepdims=True))
                acc = jnp.exp2(m_old - m) * acc_old + _dot(vt, jnp.exp2(s - m).astype(BF16))
            new_state.append((m, acc))
        return tuple(new_state)

    def pair(jj, state):
        a = pl.multiple_of(2 * jj * tq, 2 * tq)
        b = pl.multiple_of(a + tq, tq)
        scores_a, scores_b = scores_of(a, False), scores_of(b, False)
        return absorb(b, scores_b, absorb(a, scores_a, state))

    def last_odd(_, state):
        start = pl.multiple_of((i - 1) * tq, tq)
        return absorb(start, scores_of(start, False), state)

    diag0 = pl.multiple_of(i * tq, tq)
    state = absorb(diag0, scores_of(diag0, True), None)
    state = lax.fori_loop(0, lax.shift_right_logical(i, 1), pair, state)
    state = lax.fori_loop(0, i & 1, last_odd, state)

    for hp in range(ATT_HEADS // 2):
        lanes = slice(hp * LANES, (hp + 1) * LANES)
        outs = [acc[:HEAD_DIM, :] / acc[HEAD_DIM:HEAD_DIM + 1, :] for _, acc in state[2 * hp:2 * hp + 2]]
        pair = jnp.concatenate(outs, axis=0).T
        z = z_ref[:, lanes].astype(F32)
        o_ref[:, lanes] = (pair * (z * _sigmoid(z))).astype(BF16)


def _flash_scratch(tq, s):
    return [pltpu.VMEM((ATT_HEADS, 2 * LANES, tq), BF16),
            pltpu.VMEM((ATT_HEADS, V_ROWS, s), BF16)]


FOX_ONES0 = ATT_HEADS * SUBLANES


def _fox_key_placement():
    pm = np.zeros((3 * LANES, LANES), np.float32)
    for h in range(ATT_HEADS):
        for p in range(3):
            pm[p * LANES + LANE_FOX_F + h, SUBLANES * h + p] = -1.0
    return jnp.asarray(pm, BF16)


def _fox_kernel(q_ref, k_ref, v_ref, z_ref, cc_ref, cr_ref, pm_ref, o_ref, aug_ref, qt_ref, vt_ref, *, tq):
    i = pl.program_id(1)
    lane = lax.broadcasted_iota(jnp.int32, (1, LANES), 1)

    @pl.when(i == 0)
    def _():
        _transpose_values(v_ref, vt_ref, tq)
        for b0 in range(0, k_ref.shape[0], tq):
            pieces = _split3(cc_ref[b0:b0 + tq, :] * LOG2E)
            aug = _dot(jnp.concatenate(pieces, axis=1).astype(BF16), pm_ref[...])
            aug_ref[b0:b0 + tq, :] = jnp.where((lane >= FOX_ONES0) & (lane < FOX_ONES0 + 3), 1.0, aug).astype(BF16)

    q0 = pl.multiple_of(i * tq, tq)
    c_hi, c_mid, c_lo = _split3(cr_ref[:, pl.ds(q0, tq)] * LOG2E)
    row = lax.broadcasted_iota(jnp.int32, (FOX_ONES0, tq), 0)
    for hp in range(ATT_HEADS // 2):
        qp = q_ref[:, hp * LANES:(hp + 1) * LANES].astype(F32).T
        for h in (2 * hp, 2 * hp + 1):
            r = LANE_FOX_F + h
            ones_rows = jnp.where((row >= SUBLANES * h) & (row < SUBLANES * h + 3), 1.0, 0.0)
            aug_rows = jnp.concatenate([ones_rows, c_hi[r:r + 1], c_mid[r:r + 1], c_lo[r:r + 1]], axis=0)
            _store_qt(qt_ref, h, qp, aug_rows)

    _flash_all_heads(i, tq, k_ref, aug_ref, z_ref, o_ref, qt_ref, vt_ref)


def _fox(p3, cs, cst, *, tq=256):
    b, s, _ = p3.shape
    tile = lambda c: pl.BlockSpec((None, tq, ATT_WIDTH), lambda bi, i, c=c: (bi, i, c))
    full = lambda c: pl.BlockSpec((None, s, ATT_WIDTH), lambda bi, i, c=c: (bi, 0, c))
    pm = _fox_key_placement()
    return pl.pallas_call(
        functools.partial(_fox_kernel, tq=tq),
        grid=(b, s // tq),
        in_specs=[tile(COL_A + 0), full(COL_A + 1), full(COL_A + 2), tile(COL_A + 3),
                  pl.BlockSpec((None, s, LANES), lambda bi, i: (bi, 0, 0)),
                  pl.BlockSpec((None, GATE_ROWS, s), lambda bi, i: (bi, 0, 0)),
                  pl.BlockSpec(pm.shape, lambda bi, i: (0, 0))],
        out_specs=pl.BlockSpec((None, tq, ATT_WIDTH), lambda bi, i: (bi, i, 0)),
        out_shape=jax.ShapeDtypeStruct((b, s, ATT_WIDTH), BF16),
        scratch_shapes=[pltpu.VMEM((s, LANES), BF16)] + _flash_scratch(tq, s),
        compiler_params=_compiler_params(2),
        name="fox",
    )(p3, p3, p3, p3, cs, cst, pm)


def _moba_constants(s, bs):
    slopes2 = np.asarray(ALIBI_SLOPES, np.float32) * np.float32(LOG2E)
    pos = np.arange(s)
    r_s = (pos % bs).astype(np.float32)[:, None]
    block_onehot = (pos[:, None] // bs == np.arange(SUBLANES)[None, :]).astype(np.float32)
    augk = np.concatenate([r_s, r_s, r_s, np.ones((s, 3), np.float32), np.zeros((s, 2), np.float32),
                           block_onehot, block_onehot, block_onehot,
                           np.zeros((s, LANES - 4 * SUBLANES), np.float32)], axis=1)
    r_t = np.arange(bs, dtype=np.float32)
    augq = []
    for h in range(ATT_HEADS):
        m_pieces = [np.full((bs,), p, np.float32) for p in _split3_host(slopes2[h])]
        augq.append(np.stack(m_pieces + list(_split3_host(-slopes2[h] * r_t))
                             + [np.zeros((bs,), np.float32)] * 2, axis=0))
    return jnp.asarray(augk, BF16), jnp.asarray(np.stack(augq), F32)


def _moba_kernel(q_ref, k_ref, v_ref, z_ref, augk_ref, augq_ref, o_ref, kmt_ref, qt_ref, vt_ref):
    i = pl.program_id(1)
    bs = MOBA_BLOCK
    nb = k_ref.shape[0] // bs

    @pl.when(i == 0)
    def _():
        _transpose_values(v_ref, vt_ref, bs)
        kms = [jnp.mean(k_ref[j * bs:(j + 1) * bs, :].astype(F32), axis=0, keepdims=True) for j in range(nb)]
        kms += [jnp.zeros((1, ATT_WIDTH), F32)] * (SUBLANES - nb)
        km_all = jnp.concatenate(kms, axis=0)
        head_of_lane = lax.broadcasted_iota(jnp.int32, (SUBLANES, ATT_WIDTH), 1) // HEAD_DIM
        for h in range(ATT_HEADS):
            kmt_ref[h * SUBLANES:(h + 1) * SUBLANES, :] = jnp.where(head_of_lane == h, km_all, 0.0)

    q_t = [q_ref[:, hp * LANES:(hp + 1) * LANES].astype(F32).T for hp in range(ATT_HEADS // 2)]
    gate_t = _dot(kmt_ref[...].astype(BF16), jnp.concatenate(q_t, axis=0).astype(BF16))
    blk = lax.broadcasted_iota(jnp.int32, (SUBLANES, bs), 0)
    valid = blk < i
    back = (i - blk).astype(F32)
    for h in range(ATT_HEADS):
        gate = jnp.where(valid, gate_t[h * SUBLANES:(h + 1) * SUBLANES, :], NEG)
        rank = jnp.zeros((SUBLANES, bs), F32)
        for r in range(1, SUBLANES):
            other = pltpu.roll(gate, r, axis=0)
            lower = blk >= r
            rank = rank + jnp.where(other > gate, 1.0, 0.0) + jnp.where((other == gate) & lower, 1.0, 0.0)
        keep = (valid & (rank < MOBA_TOPK)) | (blk == i)
        bias = jnp.where(keep, -(ALIBI_SLOPES[h] * LOG2E * bs) * back, NEG)
        aug_rows = jnp.concatenate([augq_ref[h]] + list(_split3(bias)), axis=0)
        _store_qt(qt_ref, h, q_t[h // 2], aug_rows)

    _flash_all_heads(i, bs, k_ref, augk_ref, z_ref, o_ref, qt_ref, vt_ref)


def _moba(p3):
    b, s, _ = p3.shape
    bs = MOBA_BLOCK
    assert s % bs == 0 and s // bs <= SUBLANES
    augk, augq = _moba_constants(s, bs)
    tile = lambda c: pl.BlockSpec((None, bs, ATT_WIDTH), lambda bi, i, c=c: (bi, i, c))
    full = lambda c: pl.BlockSpec((None, s, ATT_WIDTH), lambda bi, i, c=c: (bi, 0, c))
    return pl.pallas_call(
        _moba_kernel,
        grid=(b, s // bs),
        in_specs=[tile(COL_B + 0), full(COL_B + 1), full(COL_B + 2), tile(COL_B + 3),
                  pl.BlockSpec(augk.shape, lambda bi, i: (0, 0)),
                  pl.BlockSpec(augq.shape, lambda bi, i: (0, 0, 0))],
        out_specs=pl.BlockSpec((None, bs, ATT_WIDTH), lambda bi, i: (bi, i, 0)),
        out_shape=jax.ShapeDtypeStruct((b, s, ATT_WIDTH), BF16),
        scratch_shapes=[pltpu.VMEM((ATT_HEADS * SUBLANES, ATT_WIDTH), F32)] + _flash_scratch(bs, s),
        compiler_params=_compiler_params(2),
        name="moba",
    )(p3, p3, p3, p3, augk, augq)


def _mlstm_kernel(qk_ref, v_ref, og_ref, z_ref, w_ref, cc_ref, cr_ref, rr_ref, hg_ref, y_ref,
                  cbuf_ref, cst_ref, mst_ref, fprev_ref, *, L):
    c = pl.program_id(1)
    halo = SUBLANES
    n_rows = qk_ref.shape[0]

    @pl.when(c == 0)
    def _():
        cbuf_ref[:, 0:halo, :] = jnp.zeros((n_rows, halo, cbuf_ref.shape[2]), F32)
        cst_ref[...] = jnp.zeros_like(cst_ref)
        mst_ref[...] = jnp.zeros_like(mst_ref)
        fprev_ref[...] = jnp.zeros_like(fprev_ref)

    @pl.when(c > 0)
    def _():
        cbuf_ref[:, 0:halo, :] = cbuf_ref[:, L:L + halo, :]

    row = lax.broadcasted_iota(jnp.int32, (L, L), 0)
    col = lax.broadcasted_iota(jnp.int32, (L, L), 1)
    tri = row >= col
    lane1 = lax.broadcasted_iota(jnp.int32, (1, LANES), 1)
    ones_col = jnp.broadcast_to(jnp.where(lane1 == 0, 1.0, 0.0), (L, LANES)).astype(BF16)

    qk_w = ML_HEADS * HEAD_DIM

    conv_out = []
    for g in range(n_rows):
        cbuf_ref[g, halo:halo + L, :] = qk_ref[g].astype(F32)
        conv = None
        for j in range(ML_CONV):
            off = halo - (ML_CONV - 1) + j
            term = cbuf_ref[g, off:off + L, :] * w_ref[j:j + 1, :]
            conv = term if conv is None else conv + term
        qk = conv * _sigmoid(conv)
        k_s = qk[:, qk_w:] * ATT_SCALE
        conv_out.append((qk[:, :qk_w].astype(BF16), k_s.astype(BF16), k_s.T))

    def lane_pair(h):
        return slice((h // 2) * LANES, (h // 2 + 1) * LANES)

    def v_lanes(h):
        return slice(h * ML_V_DIM, (h + 1) * ML_V_DIM)

    all_chains = [(h, g) for h in range(ML_HEADS) for g in range(n_rows)]
    for c0 in range(0, len(all_chains), ML_CHAINS):
        ch = all_chains[c0:c0 + ML_CHAINS]
        ks = range(len(ch))
        qh = [jnp.where(_head_lane_mask(h % 2), conv_out[g][0][:, lane_pair(h)], jnp.zeros((L, LANES), BF16))
              for h, g in ch]
        f_c = [cc_ref[g, :, LANE_ML_F + h:LANE_ML_F + h + 1] for h, g in ch]
        g_r = [cr_ref[g, LANE_ML_F + h:LANE_ML_F + h + 1, :]
               - rr_ref[g, LANE_ML_I + h:LANE_ML_I + h + 1, :] for h, g in ch]
        f0 = [fprev_ref[g, 0:1, LANE_ML_F + h:LANE_ML_F + h + 1] for h, g in ch]
        m_prev = [mst_ref[g, h, 0:1, 0:1] for h, g in ch]
        qk_h = [_dot_nt(qh[k], conv_out[g][1][:, lane_pair(h)]) for k, (h, g) in enumerate(ch)]
        c_prev = [cst_ref[g, h // 2] for h, g in ch]
        inter = [_dot(qh[k], c_prev[k].astype(BF16)) for k in ks]
        d_intra = [jnp.where(tri, f_c[k] - g_r[k], NEG) for k in ks]
        d_inter = [f_c[k] - f0[k] + m_prev[k] for k in ks]
        m_t = [jnp.maximum(d_inter[k], jnp.max(d_intra[k], axis=1, keepdims=True)) for k in ks]
        w_intra = [jnp.exp(d_intra[k] - m_t[k]) for k in ks]
        w_inter = [jnp.exp(d_inter[k] - m_t[k]) for k in ks]
        s_mat = [(qk_h[k] * w_intra[k]).astype(BF16) for k in ks]
        v_aug = [jnp.concatenate([v_ref[g, :, v_lanes(h)], ones_col], axis=1) for h, g in ch]
        num = [_dot(s_mat[k], v_aug[k]) + w_inter[k] * inter[k] for k in ks]
        hv = [num[k][:, :ML_V_DIM] / jnp.maximum(jnp.abs(num[k][:, ML_V_DIM:ML_V_DIM + 1]), jnp.exp(-m_t[k]))
              for k in ks]

        f_last = [f_c[k][L - 1:L, :] for k in ks]
        d_state = [f_last[k] - g_r[k] for k in ks]
        m_new = [jnp.maximum(f_last[k] - f0[k] + m_prev[k], jnp.max(d_state[k], axis=1, keepdims=True))
                 for k in ks]
        w_prev = [jnp.exp(f_last[k] - f0[k] + m_prev[k] - m_new[k]) for k in ks]
        kw = [(conv_out[g][2][h * HEAD_DIM:(h + 1) * HEAD_DIM, :] * jnp.exp(d_state[k] - m_new[k])).astype(BF16)
              for k, (h, g) in enumerate(ch)]
        for k, (h, g) in enumerate(ch):
            rows = slice((h % 2) * HEAD_DIM, (h % 2 + 1) * HEAD_DIM)
            cst_ref[g, h // 2, rows, :] = w_prev[k] * c_prev[k][rows, :] + _dot(kw[k], v_aug[k])
            mst_ref[g, h] = jnp.broadcast_to(m_new[k], mst_ref.shape[2:])

        ho = [hv[k] * _sigmoid(og_ref[g, :, v_lanes(h)].astype(F32)) for k, (h, g) in enumerate(ch)]
        yn = [ho[k] * lax.rsqrt(jnp.mean(ho[k] * ho[k], axis=1, keepdims=True) + RMS_EPS) * hg_ref[:, v_lanes(h)]
              for k, (h, g) in enumerate(ch)]
        for k, (h, g) in enumerate(ch):
            z = z_ref[g, :, v_lanes(h)].astype(F32)
            y_ref[g, :, v_lanes(h)] = (yn[k] * (z * _sigmoid(z))).astype(BF16)

    fprev_ref[...] = cc_ref[:, L - 1:L, :]


def _mlstm(p3, conv_w, cs, cst, rawt, head_g, *, L=256, n_rows=ML_ROWS):
    b, s, _ = p3.shape
    n_rows = min(n_rows, b)
    w512 = ML_HEADS * ML_V_DIM
    tile = lambda c: pl.BlockSpec((n_rows, L, w512), lambda bi, i, c=c: (bi, i, c))
    rowt = pl.BlockSpec((n_rows, GATE_ROWS, L), lambda bi, i: (bi, 0, i))
    return pl.pallas_call(
        functools.partial(_mlstm_kernel, L=L),
        grid=(b // n_rows, s // L),
        in_specs=[tile(COL_C + 0), tile(COL_C + 1), tile(COL_C + 2), tile(COL_C + 3),
                  pl.BlockSpec((ML_CONV, w512), lambda bi, i: (0, 0)),
                  pl.BlockSpec((n_rows, L, LANES), lambda bi, i: (bi, i, 0)),
                  rowt, rowt,
                  pl.BlockSpec((1, w512), lambda bi, i: (0, 0))],
        out_specs=pl.BlockSpec((n_rows, L, w512), lambda bi, i: (bi, i, 0)),
        out_shape=jax.ShapeDtypeStruct((b, s, w512), BF16),
        scratch_shapes=[pltpu.VMEM((n_rows, L + SUBLANES, w512), F32),
                        pltpu.VMEM((n_rows, ML_HEADS // 2, 2 * HEAD_DIM, 2 * ML_V_DIM), F32),
                        pltpu.VMEM((n_rows, ML_HEADS, SUBLANES, LANES), F32),
                        pltpu.VMEM((n_rows, 1, LANES), F32)],
        compiler_params=_compiler_params(2),
        name="mlstm",
    )(p3, p3, p3, p3, conv_w, cs, cst, rawt, head_g)


def _merge_kernel(ya_ref, yb_ref, yc_ref, g_ref, x_ref, wb_ref, wo_ref, fg_ref, o_ref, *, final):
    d = x_ref.shape[1]
    merged = None
    for n, y_ref in enumerate((ya_ref, yb_ref, yc_ref)):
        gate = _sigmoid(g_ref[:, n * d:(n + 1) * d].astype(F32))
        term = gate * _dot(y_ref[...], wb_ref[n])
        merged = term if merged is None else merged + term
    out = x_ref[...] + _dot(merged.astype(BF16), wo_ref[...])
    if final:
        out = out * lax.rsqrt(jnp.mean(out * out, axis=-1, keepdims=True) + RMS_EPS) * fg_ref[...]
    o_ref[...] = out


def _merge(ya, yb, yc, p2, x2, wb, wo, fg, *, final, tm=512):
    t, d = x2.shape
    w = ya.shape[1]
    ytile = pl.BlockSpec((tm, w), lambda i: (i, 0))
    return pl.pallas_call(
        functools.partial(_merge_kernel, final=final),
        grid=(t // tm,),
        in_specs=[ytile, ytile, ytile,
                  pl.BlockSpec((tm, N_BRANCHES * d), lambda i: (i, GATE_COL0 // (N_BRANCHES * d))),
                  pl.BlockSpec((tm, d), lambda i: (i, 0)),
                  pl.BlockSpec((N_BRANCHES, w, d), lambda i: (0, 0, 0)),
                  pl.BlockSpec((d, d), lambda i: (0, 0)),
                  pl.BlockSpec((1, d), lambda i: (0, 0))],
        out_specs=pl.BlockSpec((tm, d), lambda i: (i, 0)),
        out_shape=jax.ShapeDtypeStruct((t, d), F32),
        compiler_params=_compiler_params(1),
        name="merge_final" if final else "merge",
    )(ya, yb, yc, p2, x2, wb, wo, fg)


def _regroup_in_weights(w):
    aw = 4 * ATT_WIDTH
    b0 = aw + ATT_HEADS
    c_i0 = b0 + aw + 4 * ML_HEADS * ML_V_DIM
    g0 = c_i0 + 2 * ML_HEADS
    qscale = ATT_SCALE * LOG2E
    col = jnp.arange(aw)
    att_scale = jnp.where(col < ATT_WIDTH, qscale, 1.0).astype(F32)
    wa = (w[:, :aw] * att_scale).astype(BF16)
    bc_scale = jnp.concatenate([att_scale, jnp.ones((c_i0 - b0 - aw,), F32)])
    wbc = (w[:, b0:c_i0] * bc_scale).astype(BF16)
    wg = w[:, g0:].astype(BF16)
    small = jnp.concatenate([w[:, aw:b0], w[:, c_i0:g0]], axis=1)
    small = jnp.pad(small, ((0, 0), (0, LANES - small.shape[1])))
    return wa, wbc, wg, small


def kernel(x, norm_g, w_in, fox_b_f, mlstm_conv_w, mlstm_b_i, mlstm_b_f, mlstm_head_g, w_branch, w_out,
           final_norm_g):
    b, s, d = x.shape
    depth = w_in.shape[0]
    x2 = x.reshape(b * s, d)
    fg = final_norm_g.reshape(1, d)
    for layer in range(depth):
        wa, wbc, wg, small = _regroup_in_weights(w_in[layer])
        assert wa.shape[1] + wbc.shape[1] + wg.shape[1] == MAIN_WIDTH
        bias = jnp.concatenate([fox_b_f[layer], mlstm_b_i[layer], mlstm_b_f[layer]])
        bias = jnp.pad(bias, (0, LANES - bias.shape[0])).reshape(1, LANES)

        p2, gs = _inproj(x2, norm_g[layer].reshape(1, d), wa, wbc, wg, small)
        p3 = p2.reshape(b, s, MAIN_WIDTH)
        cs, cst, rawt = _gates(gs.reshape(b, s, LANES), bias)
        ya = _fox(p3, cs, cst)
        yb = _moba(p3)
        yc = _mlstm(p3, mlstm_conv_w[layer], cs, cst, rawt, mlstm_head_g[layer].reshape(1, -1))
        w512 = ya.shape[-1]
        x2 = _merge(ya.reshape(b * s, w512), yb.reshape(b * s, w512), yc.reshape(b * s, w512), p2, x2,
                    w_branch[layer].astype(BF16), w_out[layer].astype(BF16), fg,
                    final=(layer == depth - 1))
    return x2.reshape(b, s, d)
```

```python
import functools

import numpy as np
import jax
import jax.numpy as jnp
from jax import lax
from jax.experimental import pallas as pl
from jax.experimental.pallas import tpu as pltpu

F32 = jnp.float32
BF16 = jnp.bfloat16

D_MODEL = 1024
HEAD_DIM = 64
ATT_HEADS = 8
ATT_WIDTH = ATT_HEADS * HEAD_DIM
MOBA_BLOCK = 256
MOBA_TOPK = 3
ML_HEADS = 4
ML_V_DIM = 128
ML_CONV = 4
ML_ROWS = 2
ML_CHAINS = 4
N_BRANCHES = 3
RMS_EPS = 1e-6
NEG = -1e30
ATT_SCALE = HEAD_DIM ** -0.5
LOG2E = float(np.log2(np.e))
V_ROWS = 80

LANES = 128
SUBLANES = 8
VMEM_LIMIT_BYTES = 52 * 1024 * 1024

COL_A = 0
COL_B = 4
COL_C = 8
GATE_COL0 = 6144
MAIN_WIDTH = GATE_COL0 + N_BRANCHES * D_MODEL
LANE_FOX_F = 0
LANE_ML_I = 8
LANE_ML_F = 12
GATE_ROWS = 16

ALIBI_SLOPES = tuple(
    float(v) for v in 2.0 ** (-8.0 * (np.arange(ATT_HEADS, dtype=np.float32) + 1.0) / ATT_HEADS))


def _dot(a, b):
    return jnp.dot(a, b, preferred_element_type=F32)


def _dot_nt(a, b):
    return lax.dot_general(a, b, (((1,), (1,)), ((), ())), preferred_element_type=F32)


def _sigmoid(x):
    return 1.0 / (1.0 + jnp.exp(-x))


def _compiler_params(n_axes):
    return pltpu.CompilerParams(dimension_semantics=("arbitrary",) * n_axes,
                                vmem_limit_bytes=VMEM_LIMIT_BYTES)


def _inproj_kernel(x_ref, g_ref, wm_ref, ws_ref, p_ref, gs_ref, hn_ref, *, rc):
    j = pl.program_id(1)
    tm = x_ref.shape[0]

    @pl.when(j == 0)
    def _():
        ws = ws_ref[...]
        wsh = ws.astype(BF16)
        ws2 = jnp.concatenate([wsh, (ws - wsh.astype(F32)).astype(BF16)], axis=1)
        for r0 in range(0, tm, rc):
            rows = slice(r0, r0 + rc)
            x = x_ref[rows, :]
            y = x * lax.rsqrt(jnp.mean(x * x, axis=-1, keepdims=True) + RMS_EPS) * g_ref[...]
            hi = y.astype(BF16)
            hn_ref[rows, :] = hi
            gs2 = _dot(hi, ws2)
            gs_ref[rows, :] = gs2[:, :LANES] + gs2[:, LANES:]
            p_ref[rows, :] = _dot(hi, wm_ref[...]).astype(BF16)

    @pl.when(j > 0)
    def _():
        p_ref[...] = _dot(hn_ref[...], wm_ref[...]).astype(BF16)


def _inproj(x2, g, wm, ws, *, tm=2048, tn=1024, rc=256):
    t, d = x2.shape
    tm = min(tm, t)
    n = wm.shape[1]
    return pl.pallas_call(
        functools.partial(_inproj_kernel, rc=rc),
        grid=(t // tm, n // tn),
        in_specs=[pl.BlockSpec((tm, d), lambda i, j: (i, 0)),
                  pl.BlockSpec((1, d), lambda i, j: (0, 0)),
                  pl.BlockSpec((d, tn), lambda i, j: (0, j)),
                  pl.BlockSpec((d, LANES), lambda i, j: (0, 0))],
        out_specs=[pl.BlockSpec((tm, tn), lambda i, j: (i, j)),
                   pl.BlockSpec((tm, LANES), lambda i, j: (i, 0))],
        out_shape=[jax.ShapeDtypeStruct((t, n), BF16),
                   jax.ShapeDtypeStruct((t, LANES), F32)],
        scratch_shapes=[pltpu.VMEM((tm, d), BF16)],
        compiler_params=_compiler_params(2),
        name="inproj",
    )(x2, g, wm, ws)


def _gates_kernel(gs_ref, bias_ref, cs_ref, cst_ref, rawt_ref, *, cb):
    s = gs_ref.shape[0]
    lane = lax.broadcasted_iota(jnp.int32, (1, LANES), 1)
    is_log_i = (lane >= LANE_ML_I) & (lane < LANE_ML_F)
    r = lax.broadcasted_iota(jnp.int32, (cb, cb), 0)
    c = lax.broadcasted_iota(jnp.int32, (cb, cb), 1)
    tri = jnp.where(r >= c, 1.0, 0.0).astype(BF16)
    carry = jnp.zeros((1, LANES), F32)
    for blk in range(s // cb):
        rows = slice(blk * cb, (blk + 1) * cb)
        g = gs_ref[rows, :] + bias_ref[...]
        log_sig = jnp.minimum(g, 0.0) - jnp.log(1.0 + jnp.exp(-jnp.abs(g)))
        raw = jnp.where(is_log_i, g, log_sig)
        hi = raw.astype(BF16)
        r1 = raw - hi.astype(F32)
        mid = r1.astype(BF16)
        lo = (r1 - mid.astype(F32)).astype(BF16)
        cs = _dot(tri, hi) + _dot(tri, mid) + _dot(tri, lo) + carry
        carry = cs[cb - 1:cb, :]
        cs_ref[rows, :] = cs
        cst_ref[:, rows] = cs.T[:GATE_ROWS, :]
        rawt_ref[:, rows] = raw.T[:GATE_ROWS, :]


def _gates(gs3, bias, *, cb=256):
    b, s, _ = gs3.shape
    col = pl.BlockSpec((None, s, LANES), lambda i: (i, 0, 0))
    row = pl.BlockSpec((None, GATE_ROWS, s), lambda i: (i, 0, 0))
    return pl.pallas_call(
        functools.partial(_gates_kernel, cb=cb),
        grid=(b,),
        in_specs=[col, pl.BlockSpec((1, LANES), lambda i: (0, 0))],
        out_specs=[col, row, row],
        out_shape=[jax.ShapeDtypeStruct((b, s, LANES), F32),
                   jax.ShapeDtypeStruct((b, GATE_ROWS, s), F32),
                   jax.ShapeDtypeStruct((b, GATE_ROWS, s), F32)],
        compiler_params=_compiler_params(1),
        name="gates",
    )(gs3, bias)


def _split3(x):
    hi = x.astype(BF16).astype(F32)
    r = x - hi
    mid = r.astype(BF16).astype(F32)
    lo = (r - mid).astype(BF16).astype(F32)
    return hi, mid, lo


def _split3_host(x):
    x = np.asarray(x, np.float32)
    hi = x.astype(BF16).astype(np.float32)
    mid = (x - hi).astype(BF16).astype(np.float32)
    lo = (x - hi - mid).astype(BF16).astype(np.float32)
    return hi, mid, lo


def _head_lane_mask(hh):
    lane = lax.broadcasted_iota(jnp.int32, (1, LANES), 1)
    return (lane >= HEAD_DIM * hh) & (lane < HEAD_DIM * (hh + 1))


def _store_qt(qt_ref, h, q_pair_t, aug_rows):
    chan = lax.broadcasted_iota(jnp.int32, (LANES, 1), 0)
    hh = h % 2
    own = (chan >= HEAD_DIM * hh) & (chan < HEAD_DIM * (hh + 1))
    pad = jnp.zeros((LANES - aug_rows.shape[0], aug_rows.shape[1]), F32)
    qt_ref[h] = jnp.concatenate([jnp.where(own, q_pair_t, 0.0), aug_rows, pad], axis=0).astype(BF16)


def _transpose_values(v_ref, vt_ref, blk):
    s = v_ref.shape[0]
    for b0 in range(0, s, blk):
        for hp in range(ATT_HEADS // 2):
            vb = v_ref[b0:b0 + blk, hp * LANES:(hp + 1) * LANES].astype(F32).T.astype(BF16)
            for hh in range(2):
                vt_ref[2 * hp + hh, 0:HEAD_DIM, b0:b0 + blk] = vb[hh * HEAD_DIM:(hh + 1) * HEAD_DIM, :]
    extra = V_ROWS - HEAD_DIM
    ones_row = jnp.where(lax.broadcasted_iota(jnp.int32, (extra, s), 0) == 0, 1.0, 0.0).astype(BF16)
    for h in range(ATT_HEADS):
        vt_ref[h, HEAD_DIM:V_ROWS, :] = ones_row


def _flash_all_heads(i, tq, k_ref, aug_ref, z_ref, o_ref, qt_ref, vt_ref):
    key = lax.broadcasted_iota(jnp.int32, (tq, tq), 0)
    qry = lax.broadcasted_iota(jnp.int32, (tq, tq), 1)
    causal = key <= qry

    def scores_of(start, diag):
        aug = aug_ref[pl.ds(start, tq), :]
        out = []
        for hp in range(ATT_HEADS // 2):
            kj = jnp.concatenate([k_ref[pl.ds(start, tq), hp * LANES:(hp + 1) * LANES], aug], axis=1)
            for h in (2 * hp, 2 * hp + 1):
                s = _dot(kj, qt_ref[h])
                out.append(jnp.where(causal, s, NEG) if diag else s)
        return out

    def absorb(start, scores, state):
        new_state = []
        for h in range(ATT_HEADS):
            vt = vt_ref[h, :, pl.ds(start, tq)]
            s = scores[h]
            if state is None:
                m = jnp.max(s, axis=0, keepdims=True)
                acc = _dot(vt, jnp.exp2(s - m).astype(BF16))
            else:
                m_old, acc_old = state[h]
                m = jnp.maximum(m_old, jnp.max(s, axis=0, keepdims=True))
                acc = jnp.exp2(m_old - m) * acc_old + _dot(vt, jnp.exp2(s - m).astype(BF16))
            new_state.append((m, acc))
        return tuple(new_state)

    def pair(jj, state):
        a = pl.multiple_of(2 * jj * tq, 2 * tq)
        b = pl.multiple_of(a + tq, tq)
        scores_a, scores_b = scores_of(a, False), scores_of(b, False)
        return absorb(b, scores_b, absorb(a, scores_a, state))

    def last_odd(_, state):
        start = pl.multiple_of((i - 1) * tq, tq)
        return absorb(start, scores_of(start, False), state)

    diag0 = pl.multiple_of(i * tq, tq)
    state = absorb(diag0, scores_of(diag0, True), None)
    state = lax.fori_loop(0, lax.shift_right_logical(i, 1), pair, state)
    state = lax.fori_loop(0, i & 1, last_odd, state)

    for hp in range(ATT_HEADS // 2):
        lanes = slice(hp * LANES, (hp + 1) * LANES)
        outs = [acc[:HEAD_DIM, :] / acc[HEAD_DIM:HEAD_DIM + 1, :] for _, acc in state[2 * hp:2 * hp + 2]]
        pair = jnp.concatenate(outs, axis=0).T
        z = z_ref[:, lanes].astype(F32)
        o_ref[:, lanes] = (pair * (z * _sigmoid(z))).astype(BF16)


def _flash_scratch(tq, s):
    return [pltpu.VMEM((ATT_HEADS, 2 * LANES, tq), BF16),
            pltpu.VMEM((ATT_HEADS, V_ROWS, s), BF16)]


FOX_ONES0 = ATT_HEADS * SUBLANES


def _fox_key_placement():
    pm = np.zeros((3 * LANES, LANES), np.float32)
    for h in range(ATT_HEADS):
        for p in range(3):
            pm[p * LANES + LANE_FOX_F + h, SUBLANES * h + p] = -1.0
    return jnp.asarray(pm, BF16)


def _fox_kernel(q_ref, k_ref, v_ref, z_ref, cc_ref, cr_ref, pm_ref, o_ref, aug_ref, qt_ref, vt_ref, *, tq):
    i = pl.program_id(1)
    lane = lax.broadcasted_iota(jnp.int32, (1, LANES), 1)

    @pl.when(i == 0)
    def _():
        _transpose_values(v_ref, vt_ref, tq)
        for b0 in range(0, k_ref.shape[0], tq):
            pieces = _split3(cc_ref[b0:b0 + tq, :] * LOG2E)
            aug = _dot(jnp.concatenate(pieces, axis=1).astype(BF16), pm_ref[...])
            aug_ref[b0:b0 + tq, :] = jnp.where((lane >= FOX_ONES0) & (lane < FOX_ONES0 + 3), 1.0, aug).astype(BF16)

    q0 = pl.multiple_of(i * tq, tq)
    c_hi, c_mid, c_lo = _split3(cr_ref[:, pl.ds(q0, tq)] * LOG2E)
    row = lax.broadcasted_iota(jnp.int32, (FOX_ONES0, tq), 0)
    for hp in range(ATT_HEADS // 2):
        qp = q_ref[:, hp * LANES:(hp + 1) * LANES].astype(F32).T
        for h in (2 * hp, 2 * hp + 1):
            r = LANE_FOX_F + h
            ones_rows = jnp.where((row >= SUBLANES * h) & (row < SUBLANES * h + 3), 1.0, 0.0)
            aug_rows = jnp.concatenate([ones_rows, c_hi[r:r + 1], c_mid[r:r + 1], c_lo[r:r + 1]], axis=0)
            _store_qt(qt_ref, h, qp, aug_rows)

    _flash_all_heads(i, tq, k_ref, aug_ref, z_ref, o_ref, qt_ref, vt_ref)


def _fox(p3, cs, cst, *, tq=256):
    b, s, _ = p3.shape
    tile = lambda c: pl.BlockSpec((None, tq, ATT_WIDTH), lambda bi, i, c=c: (bi, i, c))
    full = lambda c: pl.BlockSpec((None, s, ATT_WIDTH), lambda bi, i, c=c: (bi, 0, c))
    pm = _fox_key_placement()
    return pl.pallas_call(
        functools.partial(_fox_kernel, tq=tq),
        grid=(b, s // tq),
        in_specs=[tile(COL_A + 0), full(COL_A + 1), full(COL_A + 2), tile(COL_A + 3),
                  pl.BlockSpec((None, s, LANES), lambda bi, i: (bi, 0, 0)),
                  pl.BlockSpec((None, GATE_ROWS, s), lambda bi, i: (bi, 0, 0)),
                  pl.BlockSpec(pm.shape, lambda bi, i: (0, 0))],
        out_specs=pl.BlockSpec((None, tq, ATT_WIDTH), lambda bi, i: (bi, i, 0)),
        out_shape=jax.ShapeDtypeStruct((b, s, ATT_WIDTH), BF16),
        scratch_shapes=[pltpu.VMEM((s, LANES), BF16)] + _flash_scratch(tq, s),
        compiler_params=_compiler_params(2),
        name="fox",
    )(p3, p3, p3, p3, cs, cst, pm)


def _moba_constants(s, bs):
    slopes2 = np.asarray(ALIBI_SLOPES, np.float32) * np.float32(LOG2E)
    pos = np.arange(s)
    r_s = (pos % bs).astype(np.float32)[:, None]
    block_onehot = (pos[:, None] // bs == np.arange(SUBLANES)[None, :]).astype(np.float32)
    augk = np.concatenate([r_s, r_s, r_s, np.ones((s, 3), np.float32), np.zeros((s, 2), np.float32),
                           block_onehot, block_onehot, block_onehot,
                           np.zeros((s, LANES - 4 * SUBLANES), np.float32)], axis=1)
    r_t = np.arange(bs, dtype=np.float32)
    augq = []
    for h in range(ATT_HEADS):
        m_pieces = [np.full((bs,), p, np.float32) for p in _split3_host(slopes2[h])]
        augq.append(np.stack(m_pieces + list(_split3_host(-slopes2[h] * r_t))
                             + [np.zeros((bs,), np.float32)] * 2, axis=0))
    return jnp.asarray(augk, BF16), jnp.asarray(np.stack(augq), F32)


def _moba_kernel(q_ref, k_ref, v_ref, z_ref, augk_ref, augq_ref, o_ref, kmt_ref, qt_ref, vt_ref):
    i = pl.program_id(1)
    bs = MOBA_BLOCK
    nb = k_ref.shape[0] // bs

    @pl.when(i == 0)
    def _():
        _transpose_values(v_ref, vt_ref, bs)
        kms = [jnp.mean(k_ref[j * bs:(j + 1) * bs, :].astype(F32), axis=0, keepdims=True) for j in range(nb)]
        kms += [jnp.zeros((1, ATT_WIDTH), F32)] * (SUBLANES - nb)
        km_all = jnp.concatenate(kms, axis=0)
        head_of_lane = lax.broadcasted_iota(jnp.int32, (SUBLANES, ATT_WIDTH), 1) // HEAD_DIM
        for h in range(ATT_HEADS):
            kmt_ref[h * SUBLANES:(h + 1) * SUBLANES, :] = jnp.where(head_of_lane == h, km_all, 0.0)

    q_t = [q_ref[:, hp * LANES:(hp + 1) * LANES].astype(F32).T for hp in range(ATT_HEADS // 2)]
    gate_t = _dot(kmt_ref[...].astype(BF16), jnp.concatenate(q_t, axis=0).astype(BF16))
    blk = lax.broadcasted_iota(jnp.int32, (SUBLANES, bs), 0)
    valid = blk < i
    back = (i - blk).astype(F32)
    for h in range(ATT_HEADS):
        gate = jnp.where(valid, gate_t[h * SUBLANES:(h + 1) * SUBLANES, :], NEG)
        rank = jnp.zeros((SUBLANES, bs), F32)
        for r in range(1, SUBLANES):
            other = pltpu.roll(gate, r, axis=0)
            lower = blk >= r
            rank = rank + jnp.where(other > gate, 1.0, 0.0) + jnp.where((other == gate) & lower, 1.0, 0.0)
        keep = (valid & (rank < MOBA_TOPK)) | (blk == i)
        bias = jnp.where(keep, -(ALIBI_SLOPES[h] * LOG2E * bs) * back, NEG)
        aug_rows = jnp.concatenate([augq_ref[h]] + list(_split3(bias)), axis=0)
        _store_qt(qt_ref, h, q_t[h // 2], aug_rows)

    _flash_all_heads(i, bs, k_ref, augk_ref, z_ref, o_ref, qt_ref, vt_ref)


def _moba(p3):
    b, s, _ = p3.shape
    bs = MOBA_BLOCK
    assert s % bs == 0 and s // bs <= SUBLANES
    augk, augq = _moba_constants(s, bs)
    tile = lambda c: pl.BlockSpec((None, bs, ATT_WIDTH), lambda bi, i, c=c: (bi, i, c))
    full = lambda c: pl.BlockSpec((None, s, ATT_WIDTH), lambda bi, i, c=c: (bi, 0, c))
    return pl.pallas_call(
        _moba_kernel,
        grid=(b, s // bs),
        in_specs=[tile(COL_B + 0), full(COL_B + 1), full(COL_B + 2), tile(COL_B + 3),
                  pl.BlockSpec(augk.shape, lambda bi, i: (0, 0)),
                  pl.BlockSpec(augq.shape, lambda bi, i: (0, 0, 0))],
        out_specs=pl.BlockSpec((None, bs, ATT_WIDTH), lambda bi, i: (bi, i, 0)),
        out_shape=jax.ShapeDtypeStruct((b, s, ATT_WIDTH), BF16),
        scratch_shapes=[pltpu.VMEM((ATT_HEADS * SUBLANES, ATT_WIDTH), F32)] + _flash_scratch(bs, s),
        compiler_params=_compiler_params(2),
        name="moba",
    )(p3, p3, p3, p3, augk, augq)


def _mlstm_kernel(qk_ref, v_ref, og_ref, z_ref, w_ref, cc_ref, cr_ref, rr_ref, hg_ref, y_ref,
                  cbuf_ref, cst_ref, mst_ref, fprev_ref, *, L):
    c = pl.program_id(1)
    halo = SUBLANES
    n_rows = qk_ref.shape[0]

    @pl.when(c == 0)
    def _():
        cbuf_ref[:, 0:halo, :] = jnp.zeros((n_rows, halo, cbuf_ref.shape[2]), F32)
        cst_ref[...] = jnp.zeros_like(cst_ref)
        mst_ref[...] = jnp.zeros_like(mst_ref)
        fprev_ref[...] = jnp.zeros_like(fprev_ref)

    @pl.when(c > 0)
    def _():
        cbuf_ref[:, 0:halo, :] = cbuf_ref[:, L:L + halo, :]

    row = lax.broadcasted_iota(jnp.int32, (L, L), 0)
    col = lax.broadcasted_iota(jnp.int32, (L, L), 1)
    tri = row >= col
    lane1 = lax.broadcasted_iota(jnp.int32, (1, LANES), 1)
    ones_col = jnp.broadcast_to(jnp.where(lane1 == 0, 1.0, 0.0), (L, LANES)).astype(BF16)

    qk_w = ML_HEADS * HEAD_DIM

    conv_out = []
    for g in range(n_rows):
        cbuf_ref[g, halo:halo + L, :] = qk_ref[g].astype(F32)
        conv = None
        for j in range(ML_CONV):
            off = halo - (ML_CONV - 1) + j
            term = cbuf_ref[g, off:off + L, :] * w_ref[j:j + 1, :]
            conv = term if conv is None else conv + term
        qk = conv * _sigmoid(conv)
        k_s = qk[:, qk_w:] * ATT_SCALE
        conv_out.append((qk[:, :qk_w].astype(BF16), k_s.astype(BF16), k_s.T))

    def lane_pair(h):
        return slice((h // 2) * LANES, (h // 2 + 1) * LANES)

    def v_lanes(h):
        return slice(h * ML_V_DIM, (h + 1) * ML_V_DIM)

    all_chains = [(h, g) for h in range(ML_HEADS) for g in range(n_rows)]
    for c0 in range(0, len(all_chains), ML_CHAINS):
        ch = all_chains[c0:c0 + ML_CHAINS]
        ks = range(len(ch))
        qh = [jnp.where(_head_lane_mask(h % 2), conv_out[g][0][:, lane_pair(h)], jnp.zeros((L, LANES), BF16))
              for h, g in ch]
        f_c = [cc_ref[g, :, LANE_ML_F + h:LANE_ML_F + h + 1] for h, g in ch]
        g_r = [cr_ref[g, LANE_ML_F + h:LANE_ML_F + h + 1, :]
               - rr_ref[g, LANE_ML_I + h:LANE_ML_I + h + 1, :] for h, g in ch]
        f0 = [fprev_ref[g, 0:1, LANE_ML_F + h:LANE_ML_F + h + 1] for h, g in ch]
        m_prev = [mst_ref[g, h, 0:1, 0:1] for h, g in ch]
        qk_h = [_dot_nt(qh[k], conv_out[g][1][:, lane_pair(h)]) for k, (h, g) in enumerate(ch)]
        c_prev = [cst_ref[g, h // 2] for h, g in ch]
        inter = [_dot(qh[k], c_prev[k].astype(BF16)) for k in ks]
        d_intra = [jnp.where(tri, f_c[k] - g_r[k], NEG) for k in ks]
        d_inter = [f_c[k] - f0[k] + m_prev[k] for k in ks]
        m_t = [jnp.maximum(d_inter[k], jnp.max(d_intra[k], axis=1, keepdims=True)) for k in ks]
        w_intra = [jnp.exp(d_intra[k] - m_t[k]) for k in ks]
        w_inter = [jnp.exp(d_inter[k] - m_t[k]) for k in ks]
        s_mat = [(qk_h[k] * w_intra[k]).astype(BF16) for k in ks]
        v_aug = [jnp.concatenate([v_ref[g, :, v_lanes(h)], ones_col], axis=1) for h, g in ch]
        num = [_dot(s_mat[k], v_aug[k]) + w_inter[k] * inter[k] for k in ks]
        hv = [num[k][:, :ML_V_DIM] / jnp.maximum(jnp.abs(num[k][:, ML_V_DIM:ML_V_DIM + 1]), jnp.exp(-m_t[k]))
              for k in ks]

        f_last = [f_c[k][L - 1:L, :] for k in ks]
        d_state = [f_last[k] - g_r[k] for k in ks]
        m_new = [jnp.maximum(f_last[k] - f0[k] + m_prev[k], jnp.max(d_state[k], axis=1, keepdims=True))
                 for k in ks]
        w_prev = [jnp.exp(f_last[k] - f0[k] + m_prev[k] - m_new[k]) for k in ks]
        kw = [(conv_out[g][2][h * HEAD_DIM:(h + 1) * HEAD_DIM, :] * jnp.exp(d_state[k] - m_new[k])).astype(BF16)
              for k, (h, g) in enumerate(ch)]
        for k, (h, g) in enumerate(ch):
            rows = slice((h % 2) * HEAD_DIM, (h % 2 + 1) * HEAD_DIM)
            cst_ref[g, h // 2, rows, :] = w_prev[k] * c_prev[k][rows, :] + _dot(kw[k], v_aug[k])
            mst_ref[g, h] = jnp.broadcast_to(m_new[k], mst_ref.shape[2:])

        ho = [hv[k] * _sigmoid(og_ref[g, :, v_lanes(h)].astype(F32)) for k, (h, g) in enumerate(ch)]
        yn = [ho[k] * lax.rsqrt(jnp.mean(ho[k] * ho[k], axis=1, keepdims=True) + RMS_EPS) * hg_ref[:, v_lanes(h)]
              for k, (h, g) in enumerate(ch)]
        for k, (h, g) in enumerate(ch):
            z = z_ref[g, :, v_lanes(h)].astype(F32)
            y_ref[g, :, v_lanes(h)] = (yn[k] * (z * _sigmoid(z))).astype(BF16)

    fprev_ref[...] = cc_ref[:, L - 1:L, :]


def _mlstm(p3, conv_w, cs, cst, rawt, head_g, *, L=256, n_rows=ML_ROWS):
    b, s, _ = p3.shape
    n_rows = min(n_rows, b)
    w512 = ML_HEADS * ML_V_DIM
    tile = lambda c: pl.BlockSpec((n_rows, L, w512), lambda bi, i, c=c: (bi, i, c))
    rowt = pl.BlockSpec((n_rows, GATE_ROWS, L), lambda bi, i: (bi, 0, i))
    return pl.pallas_call(
        functools.partial(_mlstm_kernel, L=L),
        grid=(b // n_rows, s // L),
        in_specs=[tile(COL_C + 0), tile(COL_C + 1), tile(COL_C + 2), tile(COL_C + 3),
                  pl.BlockSpec((ML_CONV, w512), lambda bi, i: (0, 0)),
                  pl.BlockSpec((n_rows, L, LANES), lambda bi, i: (bi, i, 0)),
                  rowt, rowt,
                  pl.BlockSpec((1, w512), lambda bi, i: (0, 0))],
        out_specs=pl.BlockSpec((n_rows, L, w512), lambda bi, i: (bi, i, 0)),
        out_shape=jax.ShapeDtypeStruct((b, s, w512), BF16),
        scratch_shapes=[pltpu.VMEM((n_rows, L + SUBLANES, w512), F32),
                        pltpu.VMEM((n_rows, ML_HEADS // 2, 2 * HEAD_DIM, 2 * ML_V_DIM), F32),
                        pltpu.VMEM((n_rows, ML_HEADS, SUBLANES, LANES), F32),
                        pltpu.VMEM((n_rows, 1, LANES), F32)],
        compiler_params=_compiler_params(2),
        name="mlstm",
    )(p3, p3, p3, p3, conv_w, cs, cst, rawt, head_g)


def _merge_kernel(ya_ref, yb_ref, yc_ref, g_ref, x_ref, wb_ref, wo_ref, fg_ref, o_ref, *, final):
    d = x_ref.shape[1]
    merged = None
    for n, y_ref in enumerate((ya_ref, yb_ref, yc_ref)):
        gate = _sigmoid(g_ref[:, n * d:(n + 1) * d].astype(F32))
        term = gate * _dot(y_ref[...], wb_ref[n])
        merged = term if merged is None else merged + term
    out = x_ref[...] + _dot(merged.astype(BF16), wo_ref[...])
    if final:
        out = out * lax.rsqrt(jnp.mean(out * out, axis=-1, keepdims=True) + RMS_EPS) * fg_ref[...]
    o_ref[...] = out


def _merge(ya, yb, yc, p2, x2, wb, wo, fg, *, final, tm=512):
    t, d = x2.shape
    w = ya.shape[1]
    ytile = pl.BlockSpec((tm, w), lambda i: (i, 0))
    return pl.pallas_call(
        functools.partial(_merge_kernel, final=final),
        grid=(t // tm,),
        in_specs=[ytile, ytile, ytile,
                  pl.BlockSpec((tm, N_BRANCHES * d), lambda i: (i, GATE_COL0 // (N_BRANCHES * d))),
                  pl.BlockSpec((tm, d), lambda i: (i, 0)),
                  pl.BlockSpec((N_BRANCHES, w, d), lambda i: (0, 0, 0)),
                  pl.BlockSpec((d, d), lambda i: (0, 0)),
                  pl.BlockSpec((1, d), lambda i: (0, 0))],
        out_specs=pl.BlockSpec((tm, d), lambda i: (i, 0)),
        out_shape=jax.ShapeDtypeStruct((t, d), F32),
        compiler_params=_compiler_params(1),
        name="merge_final" if final else "merge",
    )(ya, yb, yc, p2, x2, wb, wo, fg)


ATT_COLS = 4 * ATT_WIDTH
FOX_GATE0 = ATT_COLS
ML_GATE0 = FOX_GATE0 + ATT_HEADS + ATT_COLS + 4 * ML_HEADS * ML_V_DIM
REGROUP_TN = 1024
SHIFT_TILE0 = (FOX_GATE0 // REGROUP_TN,
               (ML_GATE0 - ATT_HEADS) // REGROUP_TN)


def _regroup_kernel(a_ref, b_ref, s_ref, o_ref):
    t = pl.program_id(1)
    tn = a_ref.shape[1]

    def emit(shift):
        if shift:
            win = jnp.concatenate([a_ref[...], b_ref[...]], axis=1)
            cols = pltpu.roll(win, win.shape[1] - shift, axis=1)[:, :tn]
        else:
            cols = a_ref[...]
        o_ref[...] = (cols * s_ref[...]).astype(BF16)

    pl.when(t < SHIFT_TILE0[0])(lambda: emit(0))
    pl.when((t >= SHIFT_TILE0[0]) & (t < SHIFT_TILE0[1]))(lambda: emit(ATT_HEADS))
    pl.when(t >= SHIFT_TILE0[1])(lambda: emit(ATT_HEADS + 2 * ML_HEADS))


def _regroup_in_weights(w_in, layer, *, rows=512):
    _, d, _ = w_in.shape
    tn = REGROUP_TN
    qscale = ATT_SCALE * LOG2E
    col = jnp.arange(MAIN_WIDTH)
    is_q = (col < ATT_WIDTH) | ((col >= ATT_COLS) & (col < ATT_COLS + ATT_WIDTH))
    scale = jnp.where(is_q, qscale, 1.0).astype(F32).reshape(1, MAIN_WIDTH)
    lanes_per_tile = tn // LANES
    main = pl.pallas_call(
        _regroup_kernel,
        grid=(d // rows, MAIN_WIDTH // tn),
        in_specs=[pl.BlockSpec((None, rows, tn), lambda i, t: (layer, i, t)),
                  pl.BlockSpec((None, rows, LANES), lambda i, t: (layer, i, lanes_per_tile * (t + 1))),
                  pl.BlockSpec((1, tn), lambda i, t: (0, t))],
        out_specs=pl.BlockSpec((rows, tn), lambda i, t: (i, t)),
        out_shape=jax.ShapeDtypeStruct((d, MAIN_WIDTH), BF16),
        compiler_params=_compiler_params(2),
        name="regroup",
    )(w_in, w_in, scale)
    small = jnp.concatenate([w_in[layer, :, FOX_GATE0:FOX_GATE0 + ATT_HEADS],
                             w_in[layer, :, ML_GATE0:ML_GATE0 + 2 * ML_HEADS]], axis=1)
    small = jnp.pad(small, ((0, 0), (0, LANES - small.shape[1])))
    return main, small


def kernel(x, norm_g, w_in, fox_b_f, mlstm_conv_w, mlstm_b_i, mlstm_b_f, mlstm_head_g, w_branch, w_out,
           final_norm_g):
    b, s, d = x.shape
    depth = w_in.shape[0]
    x2 = x.reshape(b * s, d)
    fg = final_norm_g.reshape(1, d)
    for layer in range(depth):
        wm, small = _regroup_in_weights(w_in, layer)
        bias = jnp.concatenate([fox_b_f[layer], mlstm_b_i[layer], mlstm_b_f[layer]])
        bias = jnp.pad(bias, (0, LANES - bias.shape[0])).reshape(1, LANES)

        p2, gs = _inproj(x2, norm_g[layer].reshape(1, d), wm, small)
        p3 = p2.reshape(b, s, MAIN_WIDTH)
        cs, cst, rawt = _gates(gs.reshape(b, s, LANES), bias)
        ya = _fox(p3, cs, cst)
        yb = _moba(p3)
        yc = _mlstm(p3, mlstm_conv_w[layer], cs, cst, rawt, mlstm_head_g[layer].reshape(1, -1))
        w512 = ya.shape[-1]
        x2 = _merge(ya.reshape(b * s, w512), yb.reshape(b * s, w512), yc.reshape(b * s, w512), p2, x2,
                    w_branch[layer].astype(BF16), w_out[layer].astype(BF16), fg,
                    final=(layer == depth - 1))
    return x2.reshape(b, s, d)
```

```python
import functools

import numpy as np
import jax
import jax.numpy as jnp
from jax import lax
from jax.experimental import pallas as pl
from jax.experimental.pallas import tpu as pltpu

F32 = jnp.float32
BF16 = jnp.bfloat16

D_MODEL = 1024
HEAD_DIM = 64
ATT_HEADS = 8
ATT_WIDTH = ATT_HEADS * HEAD_DIM
MOBA_BLOCK = 256
MOBA_TOPK = 3
ML_HEADS = 4
ML_V_DIM = 128
ML_CONV = 4
ML_ROWS = 2
ML_CHAINS = 4
N_BRANCHES = 3
RMS_EPS = 1e-6
NEG = -1e30
ATT_SCALE = HEAD_DIM ** -0.5
LOG2E = float(np.log2(np.e))
V_ROWS = 80

LANES = 128
SUBLANES = 8
VMEM_LIMIT_BYTES = 52 * 1024 * 1024

COL_A = 0
COL_B = 4
COL_C = 8
GATE_COL0 = 6144
MAIN_WIDTH = GATE_COL0 + N_BRANCHES * D_MODEL
LANE_FOX_F = 0
LANE_ML_I = 8
LANE_ML_F = 12
GATE_ROWS = 16

ALIBI_SLOPES = tuple(
    float(v) for v in 2.0 ** (-8.0 * (np.arange(ATT_HEADS, dtype=np.float32) + 1.0) / ATT_HEADS))


def _dot(a, b):
    return jnp.dot(a, b, preferred_element_type=F32)


def _dot_nt(a, b):
    return lax.dot_general(a, b, (((1,), (1,)), ((), ())), preferred_element_type=F32)


def _sigmoid(x):
    return 1.0 / (1.0 + jnp.exp(-x))


def _compiler_params(n_axes):
    return pltpu.CompilerParams(dimension_semantics=("arbitrary",) * n_axes,
                                vmem_limit_bytes=VMEM_LIMIT_BYTES)


def _inproj_kernel(x_ref, g_ref, wm_ref, ws_ref, p_ref, gs_ref, hn_ref, *, rc):
    j = pl.program_id(1)
    tm = x_ref.shape[0]

    @pl.when(j == 0)
    def _():
        ws = ws_ref[...]
        wsh = ws.astype(BF16)
        ws2 = jnp.concatenate([wsh, (ws - wsh.astype(F32)).astype(BF16)], axis=0)
        for r0 in range(0, tm, rc):
            rows = slice(r0, r0 + rc)
            x = x_ref[rows, :]
            y = x * lax.rsqrt(jnp.mean(x * x, axis=-1, keepdims=True) + RMS_EPS) * g_ref[...]
            hi = y.astype(BF16)
            hn_ref[rows, :] = hi
            gs2 = _dot_nt(hi, ws2)
            gs_ref[rows, :] = gs2[:, :LANES] + gs2[:, LANES:]
            p_ref[rows, :] = _dot_nt(hi, wm_ref[...]).astype(BF16)

    @pl.when(j > 0)
    def _():
        p_ref[...] = _dot_nt(hn_ref[...], wm_ref[...]).astype(BF16)


def _inproj(x2, g, wm, ws, *, tm=2048, tn=1024, rc=256):
    t, d = x2.shape
    tm = min(tm, t)
    n = wm.shape[0]
    return pl.pallas_call(
        functools.partial(_inproj_kernel, rc=rc),
        grid=(t // tm, n // tn),
        in_specs=[pl.BlockSpec((tm, d), lambda i, j: (i, 0)),
                  pl.BlockSpec((1, d), lambda i, j: (0, 0)),
                  pl.BlockSpec((tn, d), lambda i, j: (j, 0)),
                  pl.BlockSpec((LANES, d), lambda i, j: (0, 0))],
        out_specs=[pl.BlockSpec((tm, tn), lambda i, j: (i, j)),
                   pl.BlockSpec((tm, LANES), lambda i, j: (i, 0))],
        out_shape=[jax.ShapeDtypeStruct((t, n), BF16),
                   jax.ShapeDtypeStruct((t, LANES), F32)],
        scratch_shapes=[pltpu.VMEM((tm, d), BF16)],
        compiler_params=_compiler_params(2),
        name="inproj",
    )(x2, g, wm, ws)


def _gates_kernel(gs_ref, bias_ref, cs_ref, cst_ref, rawt_ref, *, cb):
    s = gs_ref.shape[0]
    lane = lax.broadcasted_iota(jnp.int32, (1, LANES), 1)
    is_log_i = (lane >= LANE_ML_I) & (lane < LANE_ML_F)
    r = lax.broadcasted_iota(jnp.int32, (cb, cb), 0)
    c = lax.broadcasted_iota(jnp.int32, (cb, cb), 1)
    tri = jnp.where(r >= c, 1.0, 0.0).astype(BF16)
    carry = jnp.zeros((1, LANES), F32)
    for blk in range(s // cb):
        rows = slice(blk * cb, (blk + 1) * cb)
        g = gs_ref[rows, :] + bias_ref[...]
        log_sig = jnp.minimum(g, 0.0) - jnp.log(1.0 + jnp.exp(-jnp.abs(g)))
        raw = jnp.where(is_log_i, g, log_sig)
        hi = raw.astype(BF16)
        r1 = raw - hi.astype(F32)
        mid = r1.astype(BF16)
        lo = (r1 - mid.astype(F32)).astype(BF16)
        cs = _dot(tri, hi) + _dot(tri, mid) + _dot(tri, lo) + carry
        carry = cs[cb - 1:cb, :]
        cs_ref[rows, :] = cs
        cst_ref[:, rows] = cs.T[:GATE_ROWS, :]
        rawt_ref[:, rows] = raw.T[:GATE_ROWS, :]


def _gates(gs3, bias, *, cb=256):
    b, s, _ = gs3.shape
    col = pl.BlockSpec((None, s, LANES), lambda i: (i, 0, 0))
    row = pl.BlockSpec((None, GATE_ROWS, s), lambda i: (i, 0, 0))
    return pl.pallas_call(
        functools.partial(_gates_kernel, cb=cb),
        grid=(b,),
        in_specs=[col, pl.BlockSpec((1, LANES), lambda i: (0, 0))],
        out_specs=[col, row, row],
        out_shape=[jax.ShapeDtypeStruct((b, s, LANES), F32),
                   jax.ShapeDtypeStruct((b, GATE_ROWS, s), F32),
                   jax.ShapeDtypeStruct((b, GATE_ROWS, s), F32)],
        compiler_params=_compiler_params(1),
        name="gates",
    )(gs3, bias)


def _split3(x):
    hi = x.astype(BF16).astype(F32)
    r = x - hi
    mid = r.astype(BF16).astype(F32)
    lo = (r - mid).astype(BF16).astype(F32)
    return hi, mid, lo


def _split3_host(x):
    x = np.asarray(x, np.float32)
    hi = x.astype(BF16).astype(np.float32)
    mid = (x - hi).astype(BF16).astype(np.float32)
    lo = (x - hi - mid).astype(BF16).astype(np.float32)
    return hi, mid, lo


def _head_lane_mask(hh):
    lane = lax.broadcasted_iota(jnp.int32, (1, LANES), 1)
    return (lane >= HEAD_DIM * hh) & (lane < HEAD_DIM * (hh + 1))


def _store_qt(qt_ref, h, q_pair_t, aug_rows):
    chan = lax.broadcasted_iota(jnp.int32, (LANES, 1), 0)
    hh = h % 2
    own = (chan >= HEAD_DIM * hh) & (chan < HEAD_DIM * (hh + 1))
    pad = jnp.zeros((LANES - aug_rows.shape[0], aug_rows.shape[1]), F32)
    qt_ref[h] = jnp.concatenate([jnp.where(own, q_pair_t, 0.0), aug_rows, pad], axis=0).astype(BF16)


def _transpose_values(v_ref, vt_ref, blk):
    s = v_ref.shape[0]
    for b0 in range(0, s, blk):
        for hp in range(ATT_HEADS // 2):
            vb = v_ref[b0:b0 + blk, hp * LANES:(hp + 1) * LANES].astype(F32).T.astype(BF16)
            for hh in range(2):
                vt_ref[2 * hp + hh, 0:HEAD_DIM, b0:b0 + blk] = vb[hh * HEAD_DIM:(hh + 1) * HEAD_DIM, :]
    extra = V_ROWS - HEAD_DIM
    ones_row = jnp.where(lax.broadcasted_iota(jnp.int32, (extra, s), 0) == 0, 1.0, 0.0).astype(BF16)
    for h in range(ATT_HEADS):
        vt_ref[h, HEAD_DIM:V_ROWS, :] = ones_row


def _flash_all_heads(i, tq, k_ref, aug_ref, z_ref, o_ref, qt_ref, vt_ref):
    key = lax.broadcasted_iota(jnp.int32, (tq, tq), 0)
    qry = lax.broadcasted_iota(jnp.int32, (tq, tq), 1)
    causal = key <= qry

    def scores_of(start, diag):
        aug = aug_ref[pl.ds(start, tq), :]
        out = []
        for hp in range(ATT_HEADS // 2):
            kj = jnp.concatenate([k_ref[pl.ds(start, tq), hp * LANES:(hp + 1) * LANES], aug], axis=1)
            for h in (2 * hp, 2 * hp + 1):
                s = _dot(kj, qt_ref[h])
                out.append(jnp.where(causal, s, NEG) if diag else s)
        return out

    def absorb(start, scores, state):
        new_state = []
        for h in range(ATT_HEADS):
            vt = vt_ref[h, :, pl.ds(start, tq)]
            s = scores[h]
            if state is None:
                m = jnp.max(s, axis=0, keepdims=True)
                acc = _dot(vt, jnp.exp2(s - m).astype(BF16))
            else:
                m_old, acc_old = state[h]
                m = jnp.maximum(m_old, jnp.max(s, axis=0, keepdims=True))
                acc = jnp.exp2(m_old - m) * acc_old + _dot(vt, jnp.exp2(s - m).astype(BF16))
            new_state.append((m, acc))
        return tuple(new_state)

    def pair(jj, state):
        a = pl.multiple_of(2 * jj * tq, 2 * tq)
        b = pl.multiple_of(a + tq, tq)
        scores_a, scores_b = scores_of(a, False), scores_of(b, False)
        return absorb(b, scores_b, absorb(a, scores_a, state))

    def last_odd(_, state):
        start = pl.multiple_of((i - 1) * tq, tq)
        return absorb(start, scores_of(start, False), state)

    diag0 = pl.multiple_of(i * tq, tq)
    state = absorb(diag0, scores_of(diag0, True), None)
    state = lax.fori_loop(0, lax.shift_right_logical(i, 1), pair, state)
    state = lax.fori_loop(0, i & 1, last_odd, state)

    for hp in range(ATT_HEADS // 2):
        lanes = slice(hp * LANES, (hp + 1) * LANES)
        outs = [acc[:HEAD_DIM, :] / acc[HEAD_DIM:HEAD_DIM + 1, :] for _, acc in state[2 * hp:2 * hp + 2]]
        pair = jnp.concatenate(outs, axis=0).T
        z = z_ref[:, lanes].astype(F32)
        o_ref[:, lanes] = (pair * (z * _sigmoid(z))).astype(BF16)


def _flash_scratch(tq, s):
    return [pltpu.VMEM((ATT_HEADS, 2 * LANES, tq), BF16),
            pltpu.VMEM((ATT_HEADS, V_ROWS, s), BF16)]


FOX_ONES0 = ATT_HEADS * SUBLANES


def _fox_key_placement():
    pm = np.zeros((3 * LANES, LANES), np.float32)
    for h in range(ATT_HEADS):
        for p in range(3):
            pm[p * LANES + LANE_FOX_F + h, SUBLANES * h + p] = -1.0
    return jnp.asarray(pm, BF16)


def _fox_kernel(q_ref, k_ref, v_ref, z_ref, cc_ref, cr_ref, pm_ref, o_ref, aug_ref, qt_ref, vt_ref, *, tq):
    i = pl.program_id(1)
    lane = lax.broadcasted_iota(jnp.int32, (1, LANES), 1)

    @pl.when(i == 0)
    def _():
        _transpose_values(v_ref, vt_ref, tq)
        for b0 in range(0, k_ref.shape[0], tq):
            pieces = _split3(cc_ref[b0:b0 + tq, :] * LOG2E)
            aug = _dot(jnp.concatenate(pieces, axis=1).astype(BF16), pm_ref[...])
            aug_ref[b0:b0 + tq, :] = jnp.where((lane >= FOX_ONES0) & (lane < FOX_ONES0 + 3), 1.0, aug).astype(BF16)

    q0 = pl.multiple_of(i * tq, tq)
    c_hi, c_mid, c_lo = _split3(cr_ref[:, pl.ds(q0, tq)] * LOG2E)
    row = lax.broadcasted_iota(jnp.int32, (FOX_ONES0, tq), 0)
    for hp in range(ATT_HEADS // 2):
        qp = q_ref[:, hp * LANES:(hp + 1) * LANES].astype(F32).T
        for h in (2 * hp, 2 * hp + 1):
            r = LANE_FOX_F + h
            ones_rows = jnp.where((row >= SUBLANES * h) & (row < SUBLANES * h + 3), 1.0, 0.0)
            aug_rows = jnp.concatenate([ones_rows, c_hi[r:r + 1], c_mid[r:r + 1], c_lo[r:r + 1]], axis=0)
            _store_qt(qt_ref, h, qp, aug_rows)

    _flash_all_heads(i, tq, k_ref, aug_ref, z_ref, o_ref, qt_ref, vt_ref)


def _fox(p3, cs, cst, *, tq=256):
    b, s, _ = p3.shape
    tile = lambda c: pl.BlockSpec((None, tq, ATT_WIDTH), lambda bi, i, c=c: (bi, i, c))
    full = lambda c: pl.BlockSpec((None, s, ATT_WIDTH), lambda bi, i, c=c: (bi, 0, c))
    pm = _fox_key_placement()
    return pl.pallas_call(
        functools.partial(_fox_kernel, tq=tq),
        grid=(b, s // tq),
        in_specs=[tile(COL_A + 0), full(COL_A + 1), full(COL_A + 2), tile(COL_A + 3),
                  pl.BlockSpec((None, s, LANES), lambda bi, i: (bi, 0, 0)),
                  pl.BlockSpec((None, GATE_ROWS, s), lambda bi, i: (bi, 0, 0)),
                  pl.BlockSpec(pm.shape, lambda bi, i: (0, 0))],
        out_specs=pl.BlockSpec((None, tq, ATT_WIDTH), lambda bi, i: (bi, i, 0)),
        out_shape=jax.ShapeDtypeStruct((b, s, ATT_WIDTH), BF16),
        scratch_shapes=[pltpu.VMEM((s, LANES), BF16)] + _flash_scratch(tq, s),
        compiler_params=_compiler_params(2),
        name="fox",
    )(p3, p3, p3, p3, cs, cst, pm)


def _moba_constants(s, bs):
    slopes2 = np.asarray(ALIBI_SLOPES, np.float32) * np.float32(LOG2E)
    pos = np.arange(s)
    r_s = (pos % bs).astype(np.float32)[:, None]
    block_onehot = (pos[:, None] // bs == np.arange(SUBLANES)[None, :]).astype(np.float32)
    augk = np.concatenate([r_s, r_s, r_s, np.ones((s, 3), np.float32), np.zeros((s, 2), np.float32),
                           block_onehot, block_onehot, block_onehot,
                           np.zeros((s, LANES - 4 * SUBLANES), np.float32)], axis=1)
    r_t = np.arange(bs, dtype=np.float32)
    augq = []
    for h in range(ATT_HEADS):
        m_pieces = [np.full((bs,), p, np.float32) for p in _split3_host(slopes2[h])]
        augq.append(np.stack(m_pieces + list(_split3_host(-slopes2[h] * r_t))
                             + [np.zeros((bs,), np.float32)] * 2, axis=0))
    return jnp.asarray(augk, BF16), jnp.asarray(np.stack(augq), F32)


def _moba_kernel(q_ref, k_ref, v_ref, z_ref, augk_ref, augq_ref, o_ref, kmt_ref, qt_ref, vt_ref):
    i = pl.program_id(1)
    bs = MOBA_BLOCK
    nb = k_ref.shape[0] // bs

    @pl.when(i == 0)
    def _():
        _transpose_values(v_ref, vt_ref, bs)
        kms = [jnp.mean(k_ref[j * bs:(j + 1) * bs, :].astype(F32), axis=0, keepdims=True) for j in range(nb)]
        kms += [jnp.zeros((1, ATT_WIDTH), F32)] * (SUBLANES - nb)
        km_all = jnp.concatenate(kms, axis=0)
        head_of_lane = lax.broadcasted_iota(jnp.int32, (SUBLANES, ATT_WIDTH), 1) // HEAD_DIM
        for h in range(ATT_HEADS):
            kmt_ref[h * SUBLANES:(h + 1) * SUBLANES, :] = jnp.where(head_of_lane == h, km_all, 0.0)

    q_t = [q_ref[:, hp * LANES:(hp + 1) * LANES].astype(F32).T for hp in range(ATT_HEADS // 2)]
    gate_t = _dot(kmt_ref[...].astype(BF16), jnp.concatenate(q_t, axis=0).astype(BF16))
    blk = lax.broadcasted_iota(jnp.int32, (SUBLANES, bs), 0)
    valid = blk < i
    back = (i - blk).astype(F32)
    for h in range(ATT_HEADS):
        gate = jnp.where(valid, gate_t[h * SUBLANES:(h + 1) * SUBLANES, :], NEG)
        rank = jnp.zeros((SUBLANES, bs), F32)
        for r in range(1, SUBLANES):
            other = pltpu.roll(gate, r, axis=0)
            lower = blk >= r
            rank = rank + jnp.where(other > gate, 1.0, 0.0) + jnp.where((other == gate) & lower, 1.0, 0.0)
        keep = (valid & (rank < MOBA_TOPK)) | (blk == i)
        bias = jnp.where(keep, -(ALIBI_SLOPES[h] * LOG2E * bs) * back, NEG)
        aug_rows = jnp.concatenate([augq_ref[h]] + list(_split3(bias)), axis=0)
        _store_qt(qt_ref, h, q_t[h // 2], aug_rows)

    _flash_all_heads(i, bs, k_ref, augk_ref, z_ref, o_ref, qt_ref, vt_ref)


def _moba(p3):
    b, s, _ = p3.shape
    bs = MOBA_BLOCK
    assert s % bs == 0 and s // bs <= SUBLANES
    augk, augq = _moba_constants(s, bs)
    tile = lambda c: pl.BlockSpec((None, bs, ATT_WIDTH), lambda bi, i, c=c: (bi, i, c))
    full = lambda c: pl.BlockSpec((None, s, ATT_WIDTH), lambda bi, i, c=c: (bi, 0, c))
    return pl.pallas_call(
        _moba_kernel,
        grid=(b, s // bs),
        in_specs=[tile(COL_B + 0), full(COL_B + 1), full(COL_B + 2), tile(COL_B + 3),
                  pl.BlockSpec(augk.shape, lambda bi, i: (0, 0)),
                  pl.BlockSpec(augq.shape, lambda bi, i: (0, 0, 0))],
        out_specs=pl.BlockSpec((None, bs, ATT_WIDTH), lambda bi, i: (bi, i, 0)),
        out_shape=jax.ShapeDtypeStruct((b, s, ATT_WIDTH), BF16),
        scratch_shapes=[pltpu.VMEM((ATT_HEADS * SUBLANES, ATT_WIDTH), F32)] + _flash_scratch(bs, s),
        compiler_params=_compiler_params(2),
        name="moba",
    )(p3, p3, p3, p3, augk, augq)


def _mlstm_kernel(qk_ref, v_ref, og_ref, z_ref, w_ref, cc_ref, cr_ref, rr_ref, hg_ref, y_ref,
                  cbuf_ref, cst_ref, mst_ref, fprev_ref, *, L):
    c = pl.program_id(1)
    halo = SUBLANES
    n_rows = qk_ref.shape[0]

    @pl.when(c == 0)
    def _():
        cbuf_ref[:, 0:halo, :] = jnp.zeros((n_rows, halo, cbuf_ref.shape[2]), F32)
        cst_ref[...] = jnp.zeros_like(cst_ref)
        mst_ref[...] = jnp.zeros_like(mst_ref)
        fprev_ref[...] = jnp.zeros_like(fprev_ref)

    @pl.when(c > 0)
    def _():
        cbuf_ref[:, 0:halo, :] = cbuf_ref[:, L:L + halo, :]

    row = lax.broadcasted_iota(jnp.int32, (L, L), 0)
    col = lax.broadcasted_iota(jnp.int32, (L, L), 1)
    tri = row >= col
    lane1 = lax.broadcasted_iota(jnp.int32, (1, LANES), 1)
    ones_col = jnp.broadcast_to(jnp.where(lane1 == 0, 1.0, 0.0), (L, LANES)).astype(BF16)

    qk_w = ML_HEADS * HEAD_DIM

    conv_out = []
    for g in range(n_rows):
        cbuf_ref[g, halo:halo + L, :] = qk_ref[g].astype(F32)
        conv = None
        for j in range(ML_CONV):
            off = halo - (ML_CONV - 1) + j
            term = cbuf_ref[g, off:off + L, :] * w_ref[j:j + 1, :]
            conv = term if conv is None else conv + term
        qk = conv * _sigmoid(conv)
        k_s = qk[:, qk_w:] * ATT_SCALE
        conv_out.append((qk[:, :qk_w].astype(BF16), k_s.astype(BF16), k_s.T))

    def lane_pair(h):
        return slice((h // 2) * LANES, (h // 2 + 1) * LANES)

    def v_lanes(h):
        return slice(h * ML_V_DIM, (h + 1) * ML_V_DIM)

    all_chains = [(h, g) for h in range(ML_HEADS) for g in range(n_rows)]
    for c0 in range(0, len(all_chains), ML_CHAINS):
        ch = all_chains[c0:c0 + ML_CHAINS]
        ks = range(len(ch))
        qh = [jnp.where(_head_lane_mask(h % 2), conv_out[g][0][:, lane_pair(h)], jnp.zeros((L, LANES), BF16))
              for h, g in ch]
        f_c = [cc_ref[g, :, LANE_ML_F + h:LANE_ML_F + h + 1] for h, g in ch]
        g_r = [cr_ref[g, LANE_ML_F + h:LANE_ML_F + h + 1, :]
               - rr_ref[g, LANE_ML_I + h:LANE_ML_I + h + 1, :] for h, g in ch]
        f0 = [fprev_ref[g, 0:1, LANE_ML_F + h:LANE_ML_F + h + 1] for h, g in ch]
        m_prev = [mst_ref[g, h, 0:1, 0:1] for h, g in ch]
        qk_h = [_dot_nt(qh[k], conv_out[g][1][:, lane_pair(h)]) for k, (h, g) in enumerate(ch)]
        c_prev = [cst_ref[g, h // 2] for h, g in ch]
        inter = [_dot(qh[k], c_prev[k].astype(BF16)) for k in ks]
        d_intra = [jnp.where(tri, f_c[k] - g_r[k], NEG) for k in ks]
        d_inter = [f_c[k] - f0[k] + m_prev[k] for k in ks]
        m_t = [jnp.maximum(d_inter[k], jnp.max(d_intra[k], axis=1, keepdims=True)) for k in ks]
        w_intra = [jnp.exp(d_intra[k] - m_t[k]) for k in ks]
        w_inter = [jnp.exp(d_inter[k] - m_t[k]) for k in ks]
        s_mat = [(qk_h[k] * w_intra[k]).astype(BF16) for k in ks]
        v_aug = [jnp.concatenate([v_ref[g, :, v_lanes(h)], ones_col], axis=1) for h, g in ch]
        num = [_dot(s_mat[k], v_aug[k]) + w_inter[k] * inter[k] for k in ks]
        hv = [num[k][:, :ML_V_DIM] / jnp.maximum(jnp.abs(num[k][:, ML_V_DIM:ML_V_DIM + 1]), jnp.exp(-m_t[k]))
              for k in ks]

        f_last = [f_c[k][L - 1:L, :] for k in ks]
        d_state = [f_last[k] - g_r[k] for k in ks]
        m_new = [jnp.maximum(f_last[k] - f0[k] + m_prev[k], jnp.max(d_state[k], axis=1, keepdims=True))
                 for k in ks]
        w_prev = [jnp.exp(f_last[k] - f0[k] + m_prev[k] - m_new[k]) for k in ks]
        kw = [(conv_out[g][2][h * HEAD_DIM:(h + 1) * HEAD_DIM, :] * jnp.exp(d_state[k] - m_new[k])).astype(BF16)
              for k, (h, g) in enumerate(ch)]
        for k, (h, g) in enumerate(ch):
            rows = slice((h % 2) * HEAD_DIM, (h % 2 + 1) * HEAD_DIM)
            cst_ref[g, h // 2, rows, :] = w_prev[k] * c_prev[k][rows, :] + _dot(kw[k], v_aug[k])
            mst_ref[g, h] = jnp.broadcast_to(m_new[k], mst_ref.shape[2:])

        ho = [hv[k] * _sigmoid(og_ref[g, :, v_lanes(h)].astype(F32)) for k, (h, g) in enumerate(ch)]
        yn = [ho[k] * lax.rsqrt(jnp.mean(ho[k] * ho[k], axis=1, keepdims=True) + RMS_EPS) * hg_ref[:, v_lanes(h)]
              for k, (h, g) in enumerate(ch)]
        for k, (h, g) in enumerate(ch):
            z = z_ref[g, :, v_lanes(h)].astype(F32)
            y_ref[g, :, v_lanes(h)] = (yn[k] * (z * _sigmoid(z))).astype(BF16)

    fprev_ref[...] = cc_ref[:, L - 1:L, :]


def _mlstm(p3, conv_w, cs, cst, rawt, head_g, *, L=256, n_rows=ML_ROWS):
    b, s, _ = p3.shape
    n_rows = min(n_rows, b)
    w512 = ML_HEADS * ML_V_DIM
    tile = lambda c: pl.BlockSpec((n_rows, L, w512), lambda bi, i, c=c: (bi, i, c))
    rowt = pl.BlockSpec((n_rows, GATE_ROWS, L), lambda bi, i: (bi, 0, i))
    return pl.pallas_call(
        functools.partial(_mlstm_kernel, L=L),
        grid=(b // n_rows, s // L),
        in_specs=[tile(COL_C + 0), tile(COL_C + 1), tile(COL_C + 2), tile(COL_C + 3),
                  pl.BlockSpec((ML_CONV, w512), lambda bi, i: (0, 0)),
                  pl.BlockSpec((n_rows, L, LANES), lambda bi, i: (bi, i, 0)),
                  rowt, rowt,
                  pl.BlockSpec((1, w512), lambda bi, i: (0, 0))],
        out_specs=pl.BlockSpec((n_rows, L, w512), lambda bi, i: (bi, i, 0)),
        out_shape=jax.ShapeDtypeStruct((b, s, w512), BF16),
        scratch_shapes=[pltpu.VMEM((n_rows, L + SUBLANES, w512), F32),
                        pltpu.VMEM((n_rows, ML_HEADS // 2, 2 * HEAD_DIM, 2 * ML_V_DIM), F32),
                        pltpu.VMEM((n_rows, ML_HEADS, SUBLANES, LANES), F32),
                        pltpu.VMEM((n_rows, 1, LANES), F32)],
        compiler_params=_compiler_params(2),
        name="mlstm",
    )(p3, p3, p3, p3, conv_w, cs, cst, rawt, head_g)


def _merge_kernel(ya_ref, yb_ref, yc_ref, g_ref, x_ref, wb_ref, wo_ref, fg_ref, o_ref, *, final):
    d = x_ref.shape[1]
    merged = None
    for n, y_ref in enumerate((ya_ref, yb_ref, yc_ref)):
        gate = _sigmoid(g_ref[:, n * d:(n + 1) * d].astype(F32))
        term = gate * _dot(y_ref[...], wb_ref[n])
        merged = term if merged is None else merged + term
    out = x_ref[...] + _dot(merged.astype(BF16), wo_ref[...])
    if final:
        out = out * lax.rsqrt(jnp.mean(out * out, axis=-1, keepdims=True) + RMS_EPS) * fg_ref[...]
    o_ref[...] = out


def _merge(ya, yb, yc, p2, x2, wb, wo, fg, *, final, tm=512):
    t, d = x2.shape
    w = ya.shape[1]
    ytile = pl.BlockSpec((tm, w), lambda i: (i, 0))
    return pl.pallas_call(
        functools.partial(_merge_kernel, final=final),
        grid=(t // tm,),
        in_specs=[ytile, ytile, ytile,
                  pl.BlockSpec((tm, N_BRANCHES * d), lambda i: (i, GATE_COL0 // (N_BRANCHES * d))),
                  pl.BlockSpec((tm, d), lambda i: (i, 0)),
                  pl.BlockSpec((N_BRANCHES, w, d), lambda i: (0, 0, 0)),
                  pl.BlockSpec((d, d), lambda i: (0, 0)),
                  pl.BlockSpec((1, d), lambda i: (0, 0))],
        out_specs=pl.BlockSpec((tm, d), lambda i: (i, 0)),
        out_shape=jax.ShapeDtypeStruct((t, d), F32),
        compiler_params=_compiler_params(1),
        name="merge_final" if final else "merge",
    )(ya, yb, yc, p2, x2, wb, wo, fg)


ATT_COLS = 4 * ATT_WIDTH
FOX_GATE0 = ATT_COLS
ML_GATE0 = FOX_GATE0 + ATT_HEADS + ATT_COLS + 4 * ML_HEADS * ML_V_DIM


def _regroup_in_weights(w_in, layer):
    wt = jnp.swapaxes(w_in, 1, 2)[layer]
    fox_gates = slice(FOX_GATE0, FOX_GATE0 + ATT_HEADS)
    ml_gates = slice(ML_GATE0, ML_GATE0 + 2 * ML_HEADS)
    main = jnp.concatenate([wt[:fox_gates.start], wt[fox_gates.stop:ml_gates.start], wt[ml_gates.stop:]], axis=0)
    qscale = ATT_SCALE * LOG2E
    row = jnp.arange(MAIN_WIDTH)
    is_q = (row < ATT_WIDTH) | ((row >= ATT_COLS) & (row < ATT_COLS + ATT_WIDTH))
    main = (main * jnp.where(is_q, qscale, 1.0).astype(F32)[:, None]).astype(BF16)
    small = jnp.concatenate([wt[fox_gates], wt[ml_gates]], axis=0)
    small = jnp.pad(small, ((0, LANES - small.shape[0]), (0, 0)))
    return main, small


def kernel(x, norm_g, w_in, fox_b_f, mlstm_conv_w, mlstm_b_i, mlstm_b_f, mlstm_head_g, w_branch, w_out,
           final_norm_g):
    b, s, d = x.shape
    depth = w_in.shape[0]
    x2 = x.reshape(b * s, d)
    fg = final_norm_g.reshape(1, d)
    for layer in range(depth):
        wm, small = _regroup_in_weights(w_in, layer)
        bias = jnp.concatenate([fox_b_f[layer], mlstm_b_i[layer], mlstm_b_f[layer]])
        bias = jnp.pad(bias, (0, LANES - bias.shape[0])).reshape(1, LANES)

        p2, gs = _inproj(x2, norm_g[layer].reshape(1, d), wm, small)
        p3 = p2.reshape(b, s, MAIN_WIDTH)
        cs, cst, rawt = _gates(gs.reshape(b, s, LANES), bias)
        ya = _fox(p3, cs, cst)
        yb = _moba(p3)
        yc = _mlstm(p3, mlstm_conv_w[layer], cs, cst, rawt, mlstm_head_g[layer].reshape(1, -1))
        w512 = ya.shape[-1]
        x2 = _merge(ya.reshape(b * s, w512), yb.reshape(b * s, w512), yc.reshape(b * s, w512), p2, x2,
                    w_branch[layer].astype(BF16), w_out[layer].astype(BF16), fg,
                    final=(layer == depth - 1))
    return x2.reshape(b, s, d)
```

```python
import functools

import numpy as np
import jax
import jax.numpy as jnp
from jax import lax
from jax.experimental import pallas as pl
from jax.experimental.pallas import tpu as pltpu

F32 = jnp.float32
BF16 = jnp.bfloat16

D_MODEL = 1024
HEAD_DIM = 64
ATT_HEADS = 8
ATT_WIDTH = ATT_HEADS * HEAD_DIM
MOBA_BLOCK = 256
MOBA_TOPK = 3
ML_HEADS = 4
ML_V_DIM = 128
ML_CONV = 4
ML_ROWS = 2
ML_CHAINS = 4
N_BRANCHES = 3
RMS_EPS = 1e-6
NEG = -1e30
ATT_SCALE = HEAD_DIM ** -0.5
LOG2E = float(np.log2(np.e))
V_ROWS = 80

LANES = 128
SUBLANES = 8
VMEM_LIMIT_BYTES = 52 * 1024 * 1024

COL_A = 0
COL_B = 4
COL_C = 8
GATE_COL0 = 6144
MAIN_WIDTH = GATE_COL0 + N_BRANCHES * D_MODEL
LANE_FOX_F = 0
LANE_ML_I = 8
LANE_ML_F = 12
GATE_ROWS = 16
ATT_COLS = 4 * ATT_WIDTH
FOX_GATE0 = ATT_COLS
ML_GATE0 = FOX_GATE0 + ATT_HEADS + ATT_COLS + 4 * ML_HEADS * ML_V_DIM

ALIBI_SLOPES = tuple(
    float(v) for v in 2.0 ** (-8.0 * (np.arange(ATT_HEADS, dtype=np.float32) + 1.0) / ATT_HEADS))


def _dot(a, b):
    return jnp.dot(a, b, preferred_element_type=F32)


def _dot_nt(a, b):
    return lax.dot_general(a, b, (((1,), (1,)), ((), ())), preferred_element_type=F32)


def _sigmoid(x):
    return 1.0 / (1.0 + jnp.exp(-x))


def _compiler_params(n_axes):
    return pltpu.CompilerParams(dimension_semantics=("arbitrary",) * n_axes,
                                vmem_limit_bytes=VMEM_LIMIT_BYTES)


def _weight_row0(j, tn):
    skipped = (jnp.where(j * tn >= FOX_GATE0, ATT_HEADS, 0)
               + jnp.where(j * tn >= ML_GATE0 - ATT_HEADS, 2 * ML_HEADS, 0))
    return pl.multiple_of(j * tn + skipped, SUBLANES)


def _inproj_kernel(x_ref, g_ref, wt_hbm, sc_ref, ws_ref, p_ref, gs_ref, hn_ref, wbuf_ref, sem, *, layer, rc):
    i, j = pl.program_id(0), pl.program_id(1)
    n_i, n_j = pl.num_programs(0), pl.num_programs(1)
    tm, tn = p_ref.shape
    step = i * n_j + j
    slot = step & 1

    def weight_copy(tile, slot):
        return pltpu.make_async_copy(wt_hbm.at[layer, pl.ds(_weight_row0(tile, tn), tn), :],
                                     wbuf_ref.at[slot], sem.at[slot])

    @pl.when(step == 0)
    def _():
        weight_copy(j, slot).start()

    weight_copy(j, slot).wait()

    @pl.when(step + 1 < n_i * n_j)
    def _():
        weight_copy(jnp.where(j + 1 < n_j, j + 1, 0), 1 - slot).start()

    def weights():
        return wbuf_ref[slot].astype(BF16)

    def emit(rows, hn, w):
        p_ref[rows, :] = (_dot_nt(hn, w) * sc_ref[...]).astype(BF16)

    @pl.when(j == 0)
    def _():
        ws = ws_ref[...]
        wsh = ws.astype(BF16)
        ws2 = jnp.concatenate([wsh, (ws - wsh.astype(F32)).astype(BF16)], axis=0)
        w = weights()
        for r0 in range(0, tm, rc):
            rows = slice(r0, r0 + rc)
            x = x_ref[rows, :]
            y = x * lax.rsqrt(jnp.mean(x * x, axis=-1, keepdims=True) + RMS_EPS) * g_ref[...]
            hi = y.astype(BF16)
            hn_ref[rows, :] = hi
            gs2 = _dot_nt(hi, ws2)
            gs_ref[rows, :] = gs2[:, :LANES] + gs2[:, LANES:]
            emit(rows, hi, w)

    @pl.when(j > 0)
    def _():
        emit(slice(None), hn_ref[...], weights())


def _inproj(x2, g, wt, layer, scale, ws, *, tm=2048, tn=1024, rc=256):
    t, d = x2.shape
    tm = min(tm, t)
    n = scale.shape[1]
    return pl.pallas_call(
        functools.partial(_inproj_kernel, layer=layer, rc=rc),
        grid=(t // tm, n // tn),
        in_specs=[pl.BlockSpec((tm, d), lambda i, j: (i, 0)),
                  pl.BlockSpec((1, d), lambda i, j: (0, 0)),
                  pl.BlockSpec(memory_space=pl.ANY),
                  pl.BlockSpec((1, tn), lambda i, j: (0, j)),
                  pl.BlockSpec((LANES, d), lambda i, j: (0, 0))],
        out_specs=[pl.BlockSpec((tm, tn), lambda i, j: (i, j)),
                   pl.BlockSpec((tm, LANES), lambda i, j: (i, 0))],
        out_shape=[jax.ShapeDtypeStruct((t, n), BF16),
                   jax.ShapeDtypeStruct((t, LANES), F32)],
        scratch_shapes=[pltpu.VMEM((tm, d), BF16),
                        pltpu.VMEM((2, tn, d), F32),
                        pltpu.SemaphoreType.DMA((2,))],
        compiler_params=_compiler_params(2),
        name="inproj",
    )(x2, g, wt, scale, ws)


def _gates_kernel(gs_ref, bias_ref, cs_ref, cst_ref, rawt_ref, *, cb):
    s = gs_ref.shape[0]
    lane = lax.broadcasted_iota(jnp.int32, (1, LANES), 1)
    is_log_i = (lane >= LANE_ML_I) & (lane < LANE_ML_F)
    r = lax.broadcasted_iota(jnp.int32, (cb, cb), 0)
    c = lax.broadcasted_iota(jnp.int32, (cb, cb), 1)
    tri = jnp.where(r >= c, 1.0, 0.0).astype(BF16)
    carry = jnp.zeros((1, LANES), F32)
    for blk in range(s // cb):
        rows = slice(blk * cb, (blk + 1) * cb)
        g = gs_ref[rows, :] + bias_ref[...]
        log_sig = jnp.minimum(g, 0.0) - jnp.log(1.0 + jnp.exp(-jnp.abs(g)))
        raw = jnp.where(is_log_i, g, log_sig)
        hi = raw.astype(BF16)
        r1 = raw - hi.astype(F32)
        mid = r1.astype(BF16)
        lo = (r1 - mid.astype(F32)).astype(BF16)
        cs = _dot(tri, hi) + _dot(tri, mid) + _dot(tri, lo) + carry
        carry = cs[cb - 1:cb, :]
        cs_ref[rows, :] = cs
        cst_ref[:, rows] = cs.T[:GATE_ROWS, :]
        rawt_ref[:, rows] = raw.T[:GATE_ROWS, :]


def _gates(gs3, bias, *, cb=256):
    b, s, _ = gs3.shape
    col = pl.BlockSpec((None, s, LANES), lambda i: (i, 0, 0))
    row = pl.BlockSpec((None, GATE_ROWS, s), lambda i: (i, 0, 0))
    return pl.pallas_call(
        functools.partial(_gates_kernel, cb=cb),
        grid=(b,),
        in_specs=[col, pl.BlockSpec((1, LANES), lambda i: (0, 0))],
        out_specs=[col, row, row],
        out_shape=[jax.ShapeDtypeStruct((b, s, LANES), F32),
                   jax.ShapeDtypeStruct((b, GATE_ROWS, s), F32),
                   jax.ShapeDtypeStruct((b, GATE_ROWS, s), F32)],
        compiler_params=_compiler_params(1),
        name="gates",
    )(gs3, bias)


def _split3(x):
    hi = x.astype(BF16).astype(F32)
    r = x - hi
    mid = r.astype(BF16).astype(F32)
    lo = (r - mid).astype(BF16).astype(F32)
    return hi, mid, lo


def _split3_host(x):
    x = np.asarray(x, np.float32)
    hi = x.astype(BF16).astype(np.float32)
    mid = (x - hi).astype(BF16).astype(np.float32)
    lo = (x - hi - mid).astype(BF16).astype(np.float32)
    return hi, mid, lo


def _head_lane_mask(hh):
    lane = lax.broadcasted_iota(jnp.int32, (1, LANES), 1)
    return (lane >= HEAD_DIM * hh) & (lane < HEAD_DIM * (hh + 1))


def _store_qt(qt_ref, h, q_pair_t, aug_rows):
    chan = lax.broadcasted_iota(jnp.int32, (LANES, 1), 0)
    hh = h % 2
    own = (chan >= HEAD_DIM * hh) & (chan < HEAD_DIM * (hh + 1))
    pad = jnp.zeros((LANES - aug_rows.shape[0], aug_rows.shape[1]), F32)
    qt_ref[h] = jnp.concatenate([jnp.where(own, q_pair_t, 0.0), aug_rows, pad], axis=0).astype(BF16)


def _transpose_values(v_ref, vt_ref, blk):
    s = v_ref.shape[0]
    for b0 in range(0, s, blk):
        for hp in range(ATT_HEADS // 2):
            vb = v_ref[b0:b0 + blk, hp * LANES:(hp + 1) * LANES].astype(F32).T.astype(BF16)
            for hh in range(2):
                vt_ref[2 * hp + hh, 0:HEAD_DIM, b0:b0 + blk] = vb[hh * HEAD_DIM:(hh + 1) * HEAD_DIM, :]
    extra = V_ROWS - HEAD_DIM
    ones_row = jnp.where(lax.broadcasted_iota(jnp.int32, (extra, s), 0) == 0, 1.0, 0.0).astype(BF16)
    for h in range(ATT_HEADS):
        vt_ref[h, HEAD_DIM:V_ROWS, :] = ones_row


def _flash_all_heads(i, tq, k_ref, aug_ref, z_ref, o_ref, qt_ref, vt_ref):
    key = lax.broadcasted_iota(jnp.int32, (tq, tq), 0)
    qry = lax.broadcasted_iota(jnp.int32, (tq, tq), 1)
    causal = key <= qry

    def scores_of(start, diag):
        aug = aug_ref[pl.ds(start, tq), :]
        out = []
        for hp in range(ATT_HEADS // 2):
            kj = jnp.concatenate([k_ref[pl.ds(start, tq), hp * LANES:(hp + 1) * LANES], aug], axis=1)
            for h in (2 * hp, 2 * hp + 1):
                s = _dot(kj, qt_ref[h])
                out.append(jnp.where(causal, s, NEG) if diag else s)
        return out

    def absorb(start, scores, state):
        new_state = []
        for h in range(ATT_HEADS):
            vt = vt_ref[h, :, pl.ds(start, tq)]
            s = scores[h]
            if state is None:
                m = jnp.max(s, axis=0, keepdims=True)
                acc = _dot(vt, jnp.exp2(s - m).astype(BF16))
            else:
                m_old, acc_old = state[h]
                m = jnp.maximum(m_old, jnp.max(s, axis=0, keepdims=True))
                acc = jnp.exp2(m_old - m) * acc_old + _dot(vt, jnp.exp2(s - m).astype(BF16))
            new_state.append((m, acc))
        return tuple(new_state)

    def pair(jj, state):
        a = pl.multiple_of(2 * jj * tq, 2 * tq)
        b = pl.multiple_of(a + tq, tq)
        scores_a, scores_b = scores_of(a, False), scores_of(b, False)
        return absorb(b, scores_b, absorb(a, scores_a, state))

    def last_odd(_, state):
        start = pl.multiple_of((i - 1) * tq, tq)
        return absorb(start, scores_of(start, False), state)

    diag0 = pl.multiple_of(i * tq, tq)
    state = absorb(diag0, scores_of(diag0, True), None)
    state = lax.fori_loop(0, lax.shift_right_logical(i, 1), pair, state)
    state = lax.fori_loop(0, i & 1, last_odd, state)

    for hp in range(ATT_HEADS // 2):
        lanes = slice(hp * LANES, (hp + 1) * LANES)
        outs = [acc[:HEAD_DIM, :] / acc[HEAD_DIM:HEAD_DIM + 1, :] for _, acc in state[2 * hp:2 * hp + 2]]
        pair = jnp.concatenate(outs, axis=0).T
        z = z_ref[:, lanes].astype(F32)
        o_ref[:, lanes] = (pair * (z * _sigmoid(z))).astype(BF16)


def _flash_scratch(tq, s):
    return [pltpu.VMEM((ATT_HEADS, 2 * LANES, tq), BF16),
            pltpu.VMEM((ATT_HEADS, V_ROWS, s), BF16)]


FOX_ONES0 = ATT_HEADS * SUBLANES


def _fox_key_placement():
    pm = np.zeros((3 * LANES, LANES), np.float32)
    for h in range(ATT_HEADS):
        for p in range(3):
            pm[p * LANES + LANE_FOX_F + h, SUBLANES * h + p] = -1.0
    return jnp.asarray(pm, BF16)


def _fox_kernel(q_ref, k_ref, v_ref, z_ref, cc_ref, cr_ref, pm_ref, o_ref, aug_ref, qt_ref, vt_ref, *, tq):
    i = pl.program_id(1)
    lane = lax.broadcasted_iota(jnp.int32, (1, LANES), 1)

    @pl.when(i == 0)
    def _():
        _transpose_values(v_ref, vt_ref, tq)
        for b0 in range(0, k_ref.shape[0], tq):
            pieces = _split3(cc_ref[b0:b0 + tq, :] * LOG2E)
            aug = _dot(jnp.concatenate(pieces, axis=1).astype(BF16), pm_ref[...])
            aug_ref[b0:b0 + tq, :] = jnp.where((lane >= FOX_ONES0) & (lane < FOX_ONES0 + 3), 1.0, aug).astype(BF16)

    q0 = pl.multiple_of(i * tq, tq)
    c_hi, c_mid, c_lo = _split3(cr_ref[:, pl.ds(q0, tq)] * LOG2E)
    row = lax.broadcasted_iota(jnp.int32, (FOX_ONES0, tq), 0)
    for hp in range(ATT_HEADS // 2):
        qp = q_ref[:, hp * LANES:(hp + 1) * LANES].astype(F32).T
        for h in (2 * hp, 2 * hp + 1):
            r = LANE_FOX_F + h
            ones_rows = jnp.where((row >= SUBLANES * h) & (row < SUBLANES * h + 3), 1.0, 0.0)
            aug_rows = jnp.concatenate([ones_rows, c_hi[r:r + 1], c_mid[r:r + 1], c_lo[r:r + 1]], axis=0)
            _store_qt(qt_ref, h, qp, aug_rows)

    _flash_all_heads(i, tq, k_ref, aug_ref, z_ref, o_ref, qt_ref, vt_ref)


def _fox(p3, cs, cst, *, tq=256):
    b, s, _ = p3.shape
    tile = lambda c: pl.BlockSpec((None, tq, ATT_WIDTH), lambda bi, i, c=c: (bi, i, c))
    full = lambda c: pl.BlockSpec((None, s, ATT_WIDTH), lambda bi, i, c=c: (bi, 0, c))
    pm = _fox_key_placement()
    return pl.pallas_call(
        functools.partial(_fox_kernel, tq=tq),
        grid=(b, s // tq),
        in_specs=[tile(COL_A + 0), full(COL_A + 1), full(COL_A + 2), tile(COL_A + 3),
                  pl.BlockSpec((None, s, LANES), lambda bi, i: (bi, 0, 0)),
                  pl.BlockSpec((None, GATE_ROWS, s), lambda bi, i: (bi, 0, 0)),
                  pl.BlockSpec(pm.shape, lambda bi, i: (0, 0))],
        out_specs=pl.BlockSpec((None, tq, ATT_WIDTH), lambda bi, i: (bi, i, 0)),
        out_shape=jax.ShapeDtypeStruct((b, s, ATT_WIDTH), BF16),
        scratch_shapes=[pltpu.VMEM((s, LANES), BF16)] + _flash_scratch(tq, s),
        compiler_params=_compiler_params(2),
        name="fox",
    )(p3, p3, p3, p3, cs, cst, pm)


def _moba_constants(s, bs):
    slopes2 = np.asarray(ALIBI_SLOPES, np.float32) * np.float32(LOG2E)
    pos = np.arange(s)
    r_s = (pos % bs).astype(np.float32)[:, None]
    block_onehot = (pos[:, None] // bs == np.arange(SUBLANES)[None, :]).astype(np.float32)
    augk = np.concatenate([r_s, r_s, r_s, np.ones((s, 3), np.float32), np.zeros((s, 2), np.float32),
                           block_onehot, block_onehot, block_onehot,
                           np.zeros((s, LANES - 4 * SUBLANES), np.float32)], axis=1)
    r_t = np.arange(bs, dtype=np.float32)
    augq = []
    for h in range(ATT_HEADS):
        m_pieces = [np.full((bs,), p, np.float32) for p in _split3_host(slopes2[h])]
        augq.append(np.stack(m_pieces + list(_split3_host(-slopes2[h] * r_t))
                             + [np.zeros((bs,), np.float32)] * 2, axis=0))
    return jnp.asarray(augk, BF16), jnp.asarray(np.stack(augq), F32)


def _moba_kernel(q_ref, k_ref, v_ref, z_ref, augk_ref, augq_ref, o_ref, kmt_ref, qt_ref, vt_ref):
    i = pl.program_id(1)
    bs = MOBA_BLOCK
    nb = k_ref.shape[0] // bs

    @pl.when(i == 0)
    def _():
        _transpose_values(v_ref, vt_ref, bs)
        kms = [jnp.mean(k_ref[j * bs:(j + 1) * bs, :].astype(F32), axis=0, keepdims=True) for j in range(nb)]
        kms += [jnp.zeros((1, ATT_WIDTH), F32)] * (SUBLANES - nb)
        km_all = jnp.concatenate(kms, axis=0)
        head_of_lane = lax.broadcasted_iota(jnp.int32, (SUBLANES, ATT_WIDTH), 1) // HEAD_DIM
        for h in range(ATT_HEADS):
            kmt_ref[h * SUBLANES:(h + 1) * SUBLANES, :] = jnp.where(head_of_lane == h, km_all, 0.0)

    q_t = [q_ref[:, hp * LANES:(hp + 1) * LANES].astype(F32).T for hp in range(ATT_HEADS // 2)]
    gate_t = _dot(kmt_ref[...].astype(BF16), jnp.concatenate(q_t, axis=0).astype(BF16))
    blk = lax.broadcasted_iota(jnp.int32, (SUBLANES, bs), 0)
    valid = blk < i
    back = (i - blk).astype(F32)
    for h in range(ATT_HEADS):
        gate = jnp.where(valid, gate_t[h * SUBLANES:(h + 1) * SUBLANES, :], NEG)
        rank = jnp.zeros((SUBLANES, bs), F32)
        for r in range(1, SUBLANES):
            other = pltpu.roll(gate, r, axis=0)
            lower = blk >= r
            rank = rank + jnp.where(other > gate, 1.0, 0.0) + jnp.where((other == gate) & lower, 1.0, 0.0)
        keep = (valid & (rank < MOBA_TOPK)) | (blk == i)
        bias = jnp.where(keep, -(ALIBI_SLOPES[h] * LOG2E * bs) * back, NEG)
        aug_rows = jnp.concatenate([augq_ref[h]] + list(_split3(bias)), axis=0)
        _store_qt(qt_ref, h, q_t[h // 2], aug_rows)

    _flash_all_heads(i, bs, k_ref, augk_ref, z_ref, o_ref, qt_ref, vt_ref)


def _moba(p3):
    b, s, _ = p3.shape
    bs = MOBA_BLOCK
    assert s % bs == 0 and s // bs <= SUBLANES
    augk, augq = _moba_constants(s, bs)
    tile = lambda c: pl.BlockSpec((None, bs, ATT_WIDTH), lambda bi, i, c=c: (bi, i, c))
    full = lambda c: pl.BlockSpec((None, s, ATT_WIDTH), lambda bi, i, c=c: (bi, 0, c))
    return pl.pallas_call(
        _moba_kernel,
        grid=(b, s // bs),
        in_specs=[tile(COL_B + 0), full(COL_B + 1), full(COL_B + 2), tile(COL_B + 3),
                  pl.BlockSpec(augk.shape, lambda bi, i: (0, 0)),
                  pl.BlockSpec(augq.shape, lambda bi, i: (0, 0, 0))],
        out_specs=pl.BlockSpec((None, bs, ATT_WIDTH), lambda bi, i: (bi, i, 0)),
        out_shape=jax.ShapeDtypeStruct((b, s, ATT_WIDTH), BF16),
        scratch_shapes=[pltpu.VMEM((ATT_HEADS * SUBLANES, ATT_WIDTH), F32)] + _flash_scratch(bs, s),
        compiler_params=_compiler_params(2),
        name="moba",
    )(p3, p3, p3, p3, augk, augq)


def _mlstm_kernel(qk_ref, v_ref, og_ref, z_ref, w_ref, cc_ref, cr_ref, rr_ref, hg_ref, y_ref,
                  cbuf_ref, cst_ref, mst_ref, fprev_ref, *, L):
    c = pl.program_id(1)
    halo = SUBLANES
    n_rows = qk_ref.shape[0]

    @pl.when(c == 0)
    def _():
        cbuf_ref[:, 0:halo, :] = jnp.zeros((n_rows, halo, cbuf_ref.shape[2]), F32)
        cst_ref[...] = jnp.zeros_like(cst_ref)
        mst_ref[...] = jnp.zeros_like(mst_ref)
        fprev_ref[...] = jnp.zeros_like(fprev_ref)

    @pl.when(c > 0)
    def _():
        cbuf_ref[:, 0:halo, :] = cbuf_ref[:, L:L + halo, :]

    row = lax.broadcasted_iota(jnp.int32, (L, L), 0)
    col = lax.broadcasted_iota(jnp.int32, (L, L), 1)
    tri = row >= col
    lane1 = lax.broadcasted_iota(jnp.int32, (1, LANES), 1)
    ones_col = jnp.broadcast_to(jnp.where(lane1 == 0, 1.0, 0.0), (L, LANES)).astype(BF16)

    qk_w = ML_HEADS * HEAD_DIM

    conv_out = []
    for g in range(n_rows):
        cbuf_ref[g, halo:halo + L, :] = qk_ref[g].astype(F32)
        conv = None
        for j in range(ML_CONV):
            off = halo - (ML_CONV - 1) + j
            term = cbuf_ref[g, off:off + L, :] * w_ref[j:j + 1, :]
            conv = term if conv is None else conv + term
        qk = conv * _sigmoid(conv)
        k_s = qk[:, qk_w:] * ATT_SCALE
        conv_out.append((qk[:, :qk_w].astype(BF16), k_s.astype(BF16), k_s.T))

    def lane_pair(h):
        return slice((h // 2) * LANES, (h // 2 + 1) * LANES)

    def v_lanes(h):
        return slice(h * ML_V_DIM, (h + 1) * ML_V_DIM)

    all_chains = [(h, g) for h in range(ML_HEADS) for g in range(n_rows)]
    for c0 in range(0, len(all_chains), ML_CHAINS):
        ch = all_chains[c0:c0 + ML_CHAINS]
        ks = range(len(ch))
        qh = [jnp.where(_head_lane_mask(h % 2), conv_out[g][0][:, lane_pair(h)], jnp.zeros((L, LANES), BF16))
              for h, g in ch]
        f_c = [cc_ref[g, :, LANE_ML_F + h:LANE_ML_F + h + 1] for h, g in ch]
        g_r = [cr_ref[g, LANE_ML_F + h:LANE_ML_F + h + 1, :]
               - rr_ref[g, LANE_ML_I + h:LANE_ML_I + h + 1, :] for h, g in ch]
        f0 = [fprev_ref[g, 0:1, LANE_ML_F + h:LANE_ML_F + h + 1] for h, g in ch]
        m_prev = [mst_ref[g, h, 0:1, 0:1] for h, g in ch]
        qk_h = [_dot_nt(qh[k], conv_out[g][1][:, lane_pair(h)]) for k, (h, g) in enumerate(ch)]
        c_prev = [cst_ref[g, h // 2] for h, g in ch]
        inter = [_dot(qh[k], c_prev[k].astype(BF16)) for k in ks]
        d_intra = [jnp.where(tri, f_c[k] - g_r[k], NEG) for k in ks]
        d_inter = [f_c[k] - f0[k] + m_prev[k] for k in ks]
        m_t = [jnp.maximum(d_inter[k], jnp.max(d_intra[k], axis=1, keepdims=True)) for k in ks]
        w_intra = [jnp.exp(d_intra[k] - m_t[k]) for k in ks]
        w_inter = [jnp.exp(d_inter[k] - m_t[k]) for k in ks]
        s_mat = [(qk_h[k] * w_intra[k]).astype(BF16) for k in ks]
        v_aug = [jnp.concatenate([v_ref[g, :, v_lanes(h)], ones_col], axis=1) for h, g in ch]
        num = [_dot(s_mat[k], v_aug[k]) + w_inter[k] * inter[k] for k in ks]
        hv = [num[k][:, :ML_V_DIM] / jnp.maximum(jnp.abs(num[k][:, ML_V_DIM:ML_V_DIM + 1]), jnp.exp(-m_t[k]))
              for k in ks]

        f_last = [f_c[k][L - 1:L, :] for k in ks]
        d_state = [f_last[k] - g_r[k] for k in ks]
        m_new = [jnp.maximum(f_last[k] - f0[k] + m_prev[k], jnp.max(d_state[k], axis=1, keepdims=True))
                 for k in ks]
        w_prev = [jnp.exp(f_last[k] - f0[k] + m_prev[k] - m_new[k]) for k in ks]
        kw = [(conv_out[g][2][h * HEAD_DIM:(h + 1) * HEAD_DIM, :] * jnp.exp(d_state[k] - m_new[k])).astype(BF16)
              for k, (h, g) in enumerate(ch)]
        for k, (h, g) in enumerate(ch):
            rows = slice((h % 2) * HEAD_DIM, (h % 2 + 1) * HEAD_DIM)
            cst_ref[g, h // 2, rows, :] = w_prev[k] * c_prev[k][rows, :] + _dot(kw[k], v_aug[k])
            mst_ref[g, h] = jnp.broadcast_to(m_new[k], mst_ref.shape[2:])

        ho = [hv[k] * _sigmoid(og_ref[g, :, v_lanes(h)].astype(F32)) for k, (h, g) in enumerate(ch)]
        yn = [ho[k] * lax.rsqrt(jnp.mean(ho[k] * ho[k], axis=1, keepdims=True) + RMS_EPS) * hg_ref[:, v_lanes(h)]
              for k, (h, g) in enumerate(ch)]
        for k, (h, g) in enumerate(ch):
            z = z_ref[g, :, v_lanes(h)].astype(F32)
            y_ref[g, :, v_lanes(h)] = (yn[k] * (z * _sigmoid(z))).astype(BF16)

    fprev_ref[...] = cc_ref[:, L - 1:L, :]


def _mlstm(p3, conv_w, cs, cst, rawt, head_g, *, L=256, n_rows=ML_ROWS):
    b, s, _ = p3.shape
    n_rows = min(n_rows, b)
    w512 = ML_HEADS * ML_V_DIM
    tile = lambda c: pl.BlockSpec((n_rows, L, w512), lambda bi, i, c=c: (bi, i, c))
    rowt = pl.BlockSpec((n_rows, GATE_ROWS, L), lambda bi, i: (bi, 0, i))
    return pl.pallas_call(
        functools.partial(_mlstm_kernel, L=L),
        grid=(b // n_rows, s // L),
        in_specs=[tile(COL_C + 0), tile(COL_C + 1), tile(COL_C + 2), tile(COL_C + 3),
                  pl.BlockSpec((ML_CONV, w512), lambda bi, i: (0, 0)),
                  pl.BlockSpec((n_rows, L, LANES), lambda bi, i: (bi, i, 0)),
                  rowt, rowt,
                  pl.BlockSpec((1, w512), lambda bi, i: (0, 0))],
        out_specs=pl.BlockSpec((n_rows, L, w512), lambda bi, i: (bi, i, 0)),
        out_shape=jax.ShapeDtypeStruct((b, s, w512), BF16),
        scratch_shapes=[pltpu.VMEM((n_rows, L + SUBLANES, w512), F32),
                        pltpu.VMEM((n_rows, ML_HEADS // 2, 2 * HEAD_DIM, 2 * ML_V_DIM), F32),
                        pltpu.VMEM((n_rows, ML_HEADS, SUBLANES, LANES), F32),
                        pltpu.VMEM((n_rows, 1, LANES), F32)],
        compiler_params=_compiler_params(2),
        name="mlstm",
    )(p3, p3, p3, p3, conv_w, cs, cst, rawt, head_g)


def _merge_kernel(ya_ref, yb_ref, yc_ref, g_ref, x_ref, wb_ref, wo_ref, fg_ref, o_ref, *, final):
    d = x_ref.shape[1]
    merged = None
    for n, y_ref in enumerate((ya_ref, yb_ref, yc_ref)):
        gate = _sigmoid(g_ref[:, n * d:(n + 1) * d].astype(F32))
        term = gate * _dot(y_ref[...], wb_ref[n])
        merged = term if merged is None else merged + term
    out = x_ref[...] + _dot(merged.astype(BF16), wo_ref[...])
    if final:
        out = out * lax.rsqrt(jnp.mean(out * out, axis=-1, keepdims=True) + RMS_EPS) * fg_ref[...]
    o_ref[...] = out


def _merge(ya, yb, yc, p2, x2, wb, wo, fg, *, final, tm=512):
    t, d = x2.shape
    w = ya.shape[1]
    ytile = pl.BlockSpec((tm, w), lambda i: (i, 0))
    return pl.pallas_call(
        functools.partial(_merge_kernel, final=final),
        grid=(t // tm,),
        in_specs=[ytile, ytile, ytile,
                  pl.BlockSpec((tm, N_BRANCHES * d), lambda i: (i, GATE_COL0 // (N_BRANCHES * d))),
                  pl.BlockSpec((tm, d), lambda i: (i, 0)),
                  pl.BlockSpec((N_BRANCHES, w, d), lambda i: (0, 0, 0)),
                  pl.BlockSpec((d, d), lambda i: (0, 0)),
                  pl.BlockSpec((1, d), lambda i: (0, 0))],
        out_specs=pl.BlockSpec((tm, d), lambda i: (i, 0)),
        out_shape=jax.ShapeDtypeStruct((t, d), F32),
        compiler_params=_compiler_params(1),
        name="merge_final" if final else "merge",
    )(ya, yb, yc, p2, x2, wb, wo, fg)


def _gate_rows_and_scale(wt, layer):
    small = jnp.concatenate([wt[layer, FOX_GATE0:FOX_GATE0 + ATT_HEADS],
                             wt[layer, ML_GATE0:ML_GATE0 + 2 * ML_HEADS]], axis=0)
    small = jnp.pad(small, ((0, LANES - small.shape[0]), (0, 0)))
    col = jnp.arange(MAIN_WIDTH)
    is_q = (col < ATT_WIDTH) | ((col >= ATT_COLS) & (col < ATT_COLS + ATT_WIDTH))
    qscale = ATT_SCALE * LOG2E
    return small, jnp.where(is_q, qscale, 1.0).astype(F32).reshape(1, MAIN_WIDTH)


def kernel(x, norm_g, w_in, fox_b_f, mlstm_conv_w, mlstm_b_i, mlstm_b_f, mlstm_head_g, w_branch, w_out,
           final_norm_g):
    b, s, d = x.shape
    depth = w_in.shape[0]
    x2 = x.reshape(b * s, d)
    fg = final_norm_g.reshape(1, d)
    wt = jnp.swapaxes(w_in, 1, 2)
    for layer in range(depth):
        small, scale = _gate_rows_and_scale(wt, layer)
        bias = jnp.concatenate([fox_b_f[layer], mlstm_b_i[layer], mlstm_b_f[layer]])
        bias = jnp.pad(bias, (0, LANES - bias.shape[0])).reshape(1, LANES)

        p2, gs = _inproj(x2, norm_g[layer].reshape(1, d), wt, layer, scale, small)
        p3 = p2.reshape(b, s, MAIN_WIDTH)
        cs, cst, rawt = _gates(gs.reshape(b, s, LANES), bias)
        ya = _fox(p3, cs, cst)
        yb = _moba(p3)
        yc = _mlstm(p3, mlstm_conv_w[layer], cs, cst, rawt, mlstm_head_g[layer].reshape(1, -1))
        w512 = ya.shape[-1]
        x2 = _merge(ya.reshape(b * s, w512), yb.reshape(b * s, w512), yc.reshape(b * s, w512), p2, x2,
                    w_branch[layer].astype(BF16), w_out[layer].astype(BF16), fg,
                    final=(layer == depth - 1))
    return x2.reshape(b, s, d)
```

```python
import functools

import numpy as np
import jax
import jax.numpy as jnp
from jax import lax
from jax.experimental import pallas as pl
from jax.experimental.pallas import tpu as pltpu

F32 = jnp.float32
BF16 = jnp.bfloat16

D_MODEL = 1024
HEAD_DIM = 64
ATT_HEADS = 8
ATT_WIDTH = ATT_HEADS * HEAD_DIM
MOBA_BLOCK = 256
MOBA_TOPK = 3
ML_HEADS = 4
ML_V_DIM = 128
ML_CONV = 4
ML_ROWS = 2
ML_CHAINS = 4
N_BRANCHES = 3
RMS_EPS = 1e-6
NEG = -1e30
ATT_SCALE = HEAD_DIM ** -0.5
LOG2E = float(np.log2(np.e))
V_ROWS = 80

LANES = 128
SUBLANES = 8
VMEM_LIMIT_BYTES = 52 * 1024 * 1024

COL_A = 0
COL_B = 4
COL_C = 8
GATE_COL0 = 6144
MAIN_WIDTH = GATE_COL0 + N_BRANCHES * D_MODEL
LANE_FOX_F = 0
LANE_ML_I = 8
LANE_ML_F = 12
GATE_ROWS = 16
ATT_COLS = 4 * ATT_WIDTH
FOX_GATE0 = ATT_COLS
ML_GATE0 = FOX_GATE0 + ATT_HEADS + ATT_COLS + 4 * ML_HEADS * ML_V_DIM

ALIBI_SLOPES = tuple(
    float(v) for v in 2.0 ** (-8.0 * (np.arange(ATT_HEADS, dtype=np.float32) + 1.0) / ATT_HEADS))


def _dot(a, b):
    return jnp.dot(a, b, preferred_element_type=F32)


def _dot_nt(a, b):
    return lax.dot_general(a, b, (((1,), (1,)), ((), ())), preferred_element_type=F32)


def _sigmoid(x):
    return 1.0 / (1.0 + jnp.exp(-x))


def _compiler_params(n_axes):
    return pltpu.CompilerParams(dimension_semantics=("arbitrary",) * n_axes,
                                vmem_limit_bytes=VMEM_LIMIT_BYTES)


def _weight_row0(j, tn):
    skipped = (jnp.where(j * tn >= FOX_GATE0, ATT_HEADS, 0)
               + jnp.where(j * tn >= ML_GATE0 - ATT_HEADS, 2 * ML_HEADS, 0))
    return pl.multiple_of(j * tn + skipped, SUBLANES)


def _inproj_kernel(x_ref, g_ref, wt_hbm, sc_ref, ws_ref, p_ref, gs_ref, hn_ref, wbuf_ref, sem, *, layer, rc):
    i, j = pl.program_id(0), pl.program_id(1)
    n_i, n_j = pl.num_programs(0), pl.num_programs(1)
    tm, tn = p_ref.shape
    step = i * n_j + j
    slot = step & 1

    def weight_copy(tile, slot):
        return pltpu.make_async_copy(wt_hbm.at[layer, pl.ds(_weight_row0(tile, tn), tn), :],
                                     wbuf_ref.at[slot], sem.at[slot])

    @pl.when(step == 0)
    def _():
        weight_copy(j, slot).start()

    weight_copy(j, slot).wait()

    @pl.when(step + 1 < n_i * n_j)
    def _():
        weight_copy(jnp.where(j + 1 < n_j, j + 1, 0), 1 - slot).start()

    def weights():
        return wbuf_ref[slot].astype(BF16)

    def emit(rows, hn, w):
        p_ref[rows, :] = (_dot_nt(hn, w) * sc_ref[...]).astype(BF16)

    @pl.when(j == 0)
    def _():
        ws = ws_ref[...]
        wsh = ws.astype(BF16)
        ws2 = jnp.concatenate([wsh, (ws - wsh.astype(F32)).astype(BF16)], axis=0)
        w = weights()
        for r0 in range(0, tm, rc):
            rows = slice(r0, r0 + rc)
            x = x_ref[rows, :]
            y = x * lax.rsqrt(jnp.mean(x * x, axis=-1, keepdims=True) + RMS_EPS) * g_ref[...]
            hi = y.astype(BF16)
            hn_ref[rows, :] = hi
            gs2 = _dot_nt(hi, ws2)
            gs_ref[rows, :] = gs2[:, :LANES] + gs2[:, LANES:]
            emit(rows, hi, w)

    @pl.when(j > 0)
    def _():
        emit(slice(None), hn_ref[...], weights())


def _inproj(x2, g, wt, layer, scale, ws, *, tm=2048, tn=1024, rc=256):
    t, d = x2.shape
    tm = min(tm, t)
    n = scale.shape[1]
    return pl.pallas_call(
        functools.partial(_inproj_kernel, layer=layer, rc=rc),
        grid=(t // tm, n // tn),
        in_specs=[pl.BlockSpec((tm, d), lambda i, j: (i, 0)),
                  pl.BlockSpec((1, d), lambda i, j: (0, 0)),
                  pl.BlockSpec(memory_space=pl.ANY),
                  pl.BlockSpec((1, tn), lambda i, j: (0, j)),
                  pl.BlockSpec((LANES, d), lambda i, j: (0, 0))],
        out_specs=[pl.BlockSpec((tm, tn), lambda i, j: (i, j)),
                   pl.BlockSpec((tm, LANES), lambda i, j: (i, 0))],
        out_shape=[jax.ShapeDtypeStruct((t, n), BF16),
                   jax.ShapeDtypeStruct((t, LANES), F32)],
        scratch_shapes=[pltpu.VMEM((tm, d), BF16),
                        pltpu.VMEM((2, tn, d), F32),
                        pltpu.SemaphoreType.DMA((2,))],
        compiler_params=_compiler_params(2),
        name="inproj",
    )(x2, g, wt, scale, ws)


def _gates_kernel(gs_ref, bias_ref, cs_ref, cst_ref, rawt_ref, *, cb):
    s = gs_ref.shape[0]
    lane = lax.broadcasted_iota(jnp.int32, (1, LANES), 1)
    is_log_i = (lane >= LANE_ML_I) & (lane < LANE_ML_F)
    r = lax.broadcasted_iota(jnp.int32, (cb, cb), 0)
    c = lax.broadcasted_iota(jnp.int32, (cb, cb), 1)
    tri = jnp.where(r >= c, 1.0, 0.0).astype(BF16)
    carry = jnp.zeros((1, LANES), F32)
    for blk in range(s // cb):
        rows = slice(blk * cb, (blk + 1) * cb)
        g = gs_ref[rows, :] + bias_ref[...]
        log_sig = jnp.minimum(g, 0.0) - jnp.log(1.0 + jnp.exp(-jnp.abs(g)))
        raw = jnp.where(is_log_i, g, log_sig)
        hi = raw.astype(BF16)
        r1 = raw - hi.astype(F32)
        mid = r1.astype(BF16)
        lo = (r1 - mid.astype(F32)).astype(BF16)
        cs = _dot(tri, hi) + _dot(tri, mid) + _dot(tri, lo) + carry
        carry = cs[cb - 1:cb, :]
        cs_ref[rows, :] = cs
        cst_ref[:, rows] = cs.T[:GATE_ROWS, :]
        rawt_ref[:, rows] = raw.T[:GATE_ROWS, :]


def _gates(gs3, bias, *, cb=256):
    b, s, _ = gs3.shape
    col = pl.BlockSpec((None, s, LANES), lambda i: (i, 0, 0))
    row = pl.BlockSpec((None, GATE_ROWS, s), lambda i: (i, 0, 0))
    return pl.pallas_call(
        functools.partial(_gates_kernel, cb=cb),
        grid=(b,),
        in_specs=[col, pl.BlockSpec((1, LANES), lambda i: (0, 0))],
        out_specs=[col, row, row],
        out_shape=[jax.ShapeDtypeStruct((b, s, LANES), F32),
                   jax.ShapeDtypeStruct((b, GATE_ROWS, s), F32),
                   jax.ShapeDtypeStruct((b, GATE_ROWS, s), F32)],
        compiler_params=_compiler_params(1),
        name="gates",
    )(gs3, bias)


def _split3(x):
    hi = x.astype(BF16).astype(F32)
    r = x - hi
    mid = r.astype(BF16).astype(F32)
    lo = (r - mid).astype(BF16).astype(F32)
    return hi, mid, lo


def _split3_host(x):
    x = np.asarray(x, np.float32)
    hi = x.astype(BF16).astype(np.float32)
    mid = (x - hi).astype(BF16).astype(np.float32)
    lo = (x - hi - mid).astype(BF16).astype(np.float32)
    return hi, mid, lo


def _head_lane_mask(hh):
    lane = lax.broadcasted_iota(jnp.int32, (1, LANES), 1)
    return (lane >= HEAD_DIM * hh) & (lane < HEAD_DIM * (hh + 1))


def _store_qt(qt_ref, h, q_pair_t, aug_rows):
    chan = lax.broadcasted_iota(jnp.int32, (LANES, 1), 0)
    hh = h % 2
    own = (chan >= HEAD_DIM * hh) & (chan < HEAD_DIM * (hh + 1))
    pad = jnp.zeros((LANES - aug_rows.shape[0], aug_rows.shape[1]), F32)
    qt_ref[h] = jnp.concatenate([jnp.where(own, q_pair_t, 0.0), aug_rows, pad], axis=0).astype(BF16)


def _transpose_values(v_ref, vt_ref, blk):
    s = v_ref.shape[0]
    for b0 in range(0, s, blk):
        for hp in range(ATT_HEADS // 2):
            vb = v_ref[b0:b0 + blk, hp * LANES:(hp + 1) * LANES].astype(F32).T.astype(BF16)
            for hh in range(2):
                vt_ref[2 * hp + hh, 0:HEAD_DIM, b0:b0 + blk] = vb[hh * HEAD_DIM:(hh + 1) * HEAD_DIM, :]
    extra = V_ROWS - HEAD_DIM
    ones_row = jnp.where(lax.broadcasted_iota(jnp.int32, (extra, s), 0) == 0, 1.0, 0.0).astype(BF16)
    for h in range(ATT_HEADS):
        vt_ref[h, HEAD_DIM:V_ROWS, :] = ones_row


def _flash_all_heads(i, tq, k_ref, aug_ref, z_ref, o_ref, qt_ref, vt_ref):
    key = lax.broadcasted_iota(jnp.int32, (tq, tq), 0)
    qry = lax.broadcasted_iota(jnp.int32, (tq, tq), 1)
    causal = key <= qry

    def scores_of(start, diag):
        aug = aug_ref[pl.ds(start, tq), :]
        out = []
        for hp in range(ATT_HEADS // 2):
            kj = jnp.concatenate([k_ref[pl.ds(start, tq), hp * LANES:(hp + 1) * LANES], aug], axis=1)
            for h in (2 * hp, 2 * hp + 1):
                s = _dot(kj, qt_ref[h])
                out.append(jnp.where(causal, s, NEG) if diag else s)
        return out

    def absorb(start, scores, state):
        new_state = []
        for h in range(ATT_HEADS):
            vt = vt_ref[h, :, pl.ds(start, tq)]
            s = scores[h]
            if state is None:
                m = jnp.max(s, axis=0, keepdims=True)
                acc = _dot(vt, jnp.exp2(s - m).astype(BF16))
            else:
                m_old, acc_old = state[h]
                m = jnp.maximum(m_old, jnp.max(s, axis=0, keepdims=True))
                acc = jnp.exp2(m_old - m) * acc_old + _dot(vt, jnp.exp2(s - m).astype(BF16))
            new_state.append((m, acc))
        return tuple(new_state)

    def pair(jj, state):
        a = pl.multiple_of(2 * jj * tq, 2 * tq)
        b = pl.multiple_of(a + tq, tq)
        scores_a, scores_b = scores_of(a, False), scores_of(b, False)
        return absorb(b, scores_b, absorb(a, scores_a, state))

    def last_odd(_, state):
        start = pl.multiple_of((i - 1) * tq, tq)
        return absorb(start, scores_of(start, False), state)

    diag0 = pl.multiple_of(i * tq, tq)
    state = absorb(diag0, scores_of(diag0, True), None)
    state = lax.fori_loop(0, lax.shift_right_logical(i, 1), pair, state)
    state = lax.fori_loop(0, i & 1, last_odd, state)

    for hp in range(ATT_HEADS // 2):
        lanes = slice(hp * LANES, (hp + 1) * LANES)
        outs = [acc[:HEAD_DIM, :] / acc[HEAD_DIM:HEAD_DIM + 1, :] for _, acc in state[2 * hp:2 * hp + 2]]
        pair = jnp.concatenate(outs, axis=0).T
        z = z_ref[:, lanes].astype(F32)
        o_ref[:, lanes] = (pair * (z * _sigmoid(z))).astype(BF16)


def _flash_scratch(tq, s):
    return [pltpu.VMEM((ATT_HEADS, 2 * LANES, tq), BF16),
            pltpu.VMEM((ATT_HEADS, V_ROWS, s), BF16)]


FOX_ONES0 = ATT_HEADS * SUBLANES


def _fox_key_placement():
    pm = np.zeros((3 * LANES, LANES), np.float32)
    for h in range(ATT_HEADS):
        for p in range(3):
            pm[p * LANES + LANE_FOX_F + h, SUBLANES * h + p] = -1.0
    return jnp.asarray(pm, BF16)


def _fox_kernel(q_ref, k_ref, v_ref, z_ref, cc_ref, cr_ref, pm_ref, o_ref, aug_ref, qt_ref, vt_ref, *, tq):
    i = pl.program_id(1)
    lane = lax.broadcasted_iota(jnp.int32, (1, LANES), 1)

    @pl.when(i == 0)
    def _():
        _transpose_values(v_ref, vt_ref, tq)
        for b0 in range(0, k_ref.shape[0], tq):
            pieces = _split3(cc_ref[b0:b0 + tq, :] * LOG2E)
            aug = _dot(jnp.concatenate(pieces, axis=1).astype(BF16), pm_ref[...])
            aug_ref[b0:b0 + tq, :] = jnp.where((lane >= FOX_ONES0) & (lane < FOX_ONES0 + 3), 1.0, aug).astype(BF16)

    q0 = pl.multiple_of(i * tq, tq)
    c_hi, c_mid, c_lo = _split3(cr_ref[:, pl.ds(q0, tq)] * LOG2E)
    row = lax.broadcasted_iota(jnp.int32, (FOX_ONES0, tq), 0)
    for hp in range(ATT_HEADS // 2):
        qp = q_ref[:, hp * LANES:(hp + 1) * LANES].astype(F32).T
        for h in (2 * hp, 2 * hp + 1):
            r = LANE_FOX_F + h
            ones_rows = jnp.where((row >= SUBLANES * h) & (row < SUBLANES * h + 3), 1.0, 0.0)
            aug_rows = jnp.concatenate([ones_rows, c_hi[r:r + 1], c_mid[r:r + 1], c_lo[r:r + 1]], axis=0)
            _store_qt(qt_ref, h, qp, aug_rows)

    _flash_all_heads(i, tq, k_ref, aug_ref, z_ref, o_ref, qt_ref, vt_ref)


def _fox(p3, cs, cst, *, tq=256):
    b, s, _ = p3.shape
    tile = lambda c: pl.BlockSpec((None, tq, ATT_WIDTH), lambda bi, i, c=c: (bi, i, c))
    full = lambda c: pl.BlockSpec((None, s, ATT_WIDTH), lambda bi, i, c=c: (bi, 0, c))
    pm = _fox_key_placement()
    return pl.pallas_call(
        functools.partial(_fox_kernel, tq=tq),
        grid=(b, s // tq),
        in_specs=[tile(COL_A + 0), full(COL_A + 1), full(COL_A + 2), tile(COL_A + 3),
                  pl.BlockSpec((None, s, LANES), lambda bi, i: (bi, 0, 0)),
                  pl.BlockSpec((None, GATE_ROWS, s), lambda bi, i: (bi, 0, 0)),
                  pl.BlockSpec(pm.shape, lambda bi, i: (0, 0))],
        out_specs=pl.BlockSpec((None, tq, ATT_WIDTH), lambda bi, i: (bi, i, 0)),
        out_shape=jax.ShapeDtypeStruct((b, s, ATT_WIDTH), BF16),
        scratch_shapes=[pltpu.VMEM((s, LANES), BF16)] + _flash_scratch(tq, s),
        compiler_params=_compiler_params(2),
        name="fox",
    )(p3, p3, p3, p3, cs, cst, pm)


def _moba_constants(s, bs):
    slopes2 = np.asarray(ALIBI_SLOPES, np.float32) * np.float32(LOG2E)
    pos = np.arange(s)
    r_s = (pos % bs).astype(np.float32)[:, None]
    block_onehot = (pos[:, None] // bs == np.arange(SUBLANES)[None, :]).astype(np.float32)
    augk = np.concatenate([r_s, r_s, r_s, np.ones((s, 3), np.float32), np.zeros((s, 2), np.float32),
                           block_onehot, block_onehot, block_onehot,
                           np.zeros((s, LANES - 4 * SUBLANES), np.float32)], axis=1)
    r_t = np.arange(bs, dtype=np.float32)
    augq = []
    for h in range(ATT_HEADS):
        m_pieces = [np.full((bs,), p, np.float32) for p in _split3_host(slopes2[h])]
        augq.append(np.stack(m_pieces + list(_split3_host(-slopes2[h] * r_t))
                             + [np.zeros((bs,), np.float32)] * 2, axis=0))
    return jnp.asarray(augk, BF16), jnp.asarray(np.stack(augq), F32)


def _moba_kernel(q_ref, k_ref, v_ref, z_ref, augk_ref, augq_ref, o_ref, kmt_ref, qt_ref, vt_ref):
    i = pl.program_id(1)
    bs = MOBA_BLOCK
    nb = k_ref.shape[0] // bs

    @pl.when(i == 0)
    def _():
        _transpose_values(v_ref, vt_ref, bs)
        kms = [jnp.mean(k_ref[j * bs:(j + 1) * bs, :].astype(F32), axis=0, keepdims=True) for j in range(nb)]
        kms += [jnp.zeros((1, ATT_WIDTH), F32)] * (SUBLANES - nb)
        km_all = jnp.concatenate(kms, axis=0)
        head_of_lane = lax.broadcasted_iota(jnp.int32, (SUBLANES, ATT_WIDTH), 1) // HEAD_DIM
        for h in range(ATT_HEADS):
            kmt_ref[h * SUBLANES:(h + 1) * SUBLANES, :] = jnp.where(head_of_lane == h, km_all, 0.0)

    q_t = [q_ref[:, hp * LANES:(hp + 1) * LANES].astype(F32).T for hp in range(ATT_HEADS // 2)]
    gate_t = _dot(kmt_ref[...].astype(BF16), jnp.concatenate(q_t, axis=0).astype(BF16))
    blk = lax.broadcasted_iota(jnp.int32, (SUBLANES, bs), 0)
    valid = blk < i
    back = (i - blk).astype(F32)
    for h in range(ATT_HEADS):
        gate = jnp.where(valid, gate_t[h * SUBLANES:(h + 1) * SUBLANES, :], NEG)
        rank = jnp.zeros((SUBLANES, bs), F32)
        for r in range(1, SUBLANES):
            other = pltpu.roll(gate, r, axis=0)
            lower = blk >= r
            rank = rank + jnp.where(other > gate, 1.0, 0.0) + jnp.where((other == gate) & lower, 1.0, 0.0)
        keep = (valid & (rank < MOBA_TOPK)) | (blk == i)
        bias = jnp.where(keep, -(ALIBI_SLOPES[h] * LOG2E * bs) * back, NEG)
        aug_rows = jnp.concatenate([augq_ref[h]] + list(_split3(bias)), axis=0)
        _store_qt(qt_ref, h, q_t[h // 2], aug_rows)

    _flash_all_heads(i, bs, k_ref, augk_ref, z_ref, o_ref, qt_ref, vt_ref)


def _moba(p3):
    b, s, _ = p3.shape
    bs = MOBA_BLOCK
    assert s % bs == 0 and s // bs <= SUBLANES
    augk, augq = _moba_constants(s, bs)
    tile = lambda c: pl.BlockSpec((None, bs, ATT_WIDTH), lambda bi, i, c=c: (bi, i, c))
    full = lambda c: pl.BlockSpec((None, s, ATT_WIDTH), lambda bi, i, c=c: (bi, 0, c))
    return pl.pallas_call(
        _moba_kernel,
        grid=(b, s // bs),
        in_specs=[tile(COL_B + 0), full(COL_B + 1), full(COL_B + 2), tile(COL_B + 3),
                  pl.BlockSpec(augk.shape, lambda bi, i: (0, 0)),
                  pl.BlockSpec(augq.shape, lambda bi, i: (0, 0, 0))],
        out_specs=pl.BlockSpec((None, bs, ATT_WIDTH), lambda bi, i: (bi, i, 0)),
        out_shape=jax.ShapeDtypeStruct((b, s, ATT_WIDTH), BF16),
        scratch_shapes=[pltpu.VMEM((ATT_HEADS * SUBLANES, ATT_WIDTH), F32)] + _flash_scratch(bs, s),
        compiler_params=_compiler_params(2),
        name="moba",
    )(p3, p3, p3, p3, augk, augq)


def _mlstm_kernel(qk_ref, v_ref, og_ref, z_ref, w_ref, cc_ref, cr_ref, rr_ref, hg_ref, y_ref,
                  cbuf_ref, cst_ref, mst_ref, fprev_ref, *, L):
    c = pl.program_id(1)
    halo = SUBLANES
    n_rows = qk_ref.shape[0]

    @pl.when(c == 0)
    def _():
        cbuf_ref[:, 0:halo, :] = jnp.zeros((n_rows, halo, cbuf_ref.shape[2]), F32)
        cst_ref[...] = jnp.zeros_like(cst_ref)
        mst_ref[...] = jnp.zeros_like(mst_ref)
        fprev_ref[...] = jnp.zeros_like(fprev_ref)

    @pl.when(c > 0)
    def _():
        cbuf_ref[:, 0:halo, :] = cbuf_ref[:, L:L + halo, :]

    row = lax.broadcasted_iota(jnp.int32, (L, L), 0)
    col = lax.broadcasted_iota(jnp.int32, (L, L), 1)
    tri = row >= col
    lane1 = lax.broadcasted_iota(jnp.int32, (1, LANES), 1)
    ones_col = jnp.broadcast_to(jnp.where(lane1 == 0, 1.0, 0.0), (L, LANES)).astype(BF16)

    qk_w = ML_HEADS * HEAD_DIM

    conv_out = []
    for g in range(n_rows):
        cbuf_ref[g, halo:halo + L, :] = qk_ref[g].astype(F32)
        conv = None
        for j in range(ML_CONV):
            off = halo - (ML_CONV - 1) + j
            term = cbuf_ref[g, off:off + L, :] * w_ref[j:j + 1, :]
            conv = term if conv is None else conv + term
        qk = conv * _sigmoid(conv)
        k_s = qk[:, qk_w:] * ATT_SCALE
        conv_out.append((qk[:, :qk_w].astype(BF16), k_s.astype(BF16), k_s.T))

    def lane_pair(h):
        return slice((h // 2) * LANES, (h // 2 + 1) * LANES)

    def v_lanes(h):
        return slice(h * ML_V_DIM, (h + 1) * ML_V_DIM)

    all_chains = [(h, g) for h in range(ML_HEADS) for g in range(n_rows)]
    for c0 in range(0, len(all_chains), ML_CHAINS):
        ch = all_chains[c0:c0 + ML_CHAINS]
        ks = range(len(ch))
        qh = [jnp.where(_head_lane_mask(h % 2), conv_out[g][0][:, lane_pair(h)], jnp.zeros((L, LANES), BF16))
              for h, g in ch]
        f_c = [cc_ref[g, :, LANE_ML_F + h:LANE_ML_F + h + 1] for h, g in ch]
        g_r = [cr_ref[g, LANE_ML_F + h:LANE_ML_F + h + 1, :]
               - rr_ref[g, LANE_ML_I + h:LANE_ML_I + h + 1, :] for h, g in ch]
        f0 = [fprev_ref[g, 0:1, LANE_ML_F + h:LANE_ML_F + h + 1] for h, g in ch]
        m_prev = [mst_ref[g, h, 0:1, 0:1] for h, g in ch]
        qk_h = [_dot_nt(qh[k], conv_out[g][1][:, lane_pair(h)]) for k, (h, g) in enumerate(ch)]
        c_prev = [cst_ref[g, h // 2] for h, g in ch]
        inter = [_dot(qh[k], c_prev[k].astype(BF16)) for k in ks]
        d_intra = [jnp.where(tri, f_c[k] - g_r[k], NEG) for k in ks]
        d_inter = [f_c[k] - f0[k] + m_prev[k] for k in ks]
        m_t = [jnp.maximum(d_inter[k], jnp.max(d_intra[k], axis=1, keepdims=True)) for k in ks]
        w_intra = [jnp.exp(d_intra[k] - m_t[k]) for k in ks]
        w_inter = [jnp.exp(d_inter[k] - m_t[k]) for k in ks]
        s_mat = [(qk_h[k] * w_intra[k]).astype(BF16) for k in ks]
        v_aug = [jnp.concatenate([v_ref[g, :, v_lanes(h)], ones_col], axis=1) for h, g in ch]
        num = [_dot(s_mat[k], v_aug[k]) + w_inter[k] * inter[k] for k in ks]
        hv = [num[k][:, :ML_V_DIM] / jnp.maximum(jnp.abs(num[k][:, ML_V_DIM:ML_V_DIM + 1]), jnp.exp(-m_t[k]))
              for k in ks]

        f_last = [f_c[k][L - 1:L, :] for k in ks]
        d_state = [f_last[k] - g_r[k] for k in ks]
        m_new = [jnp.maximum(f_last[k] - f0[k] + m_prev[k], jnp.max(d_state[k], axis=1, keepdims=True))
                 for k in ks]
        w_prev = [jnp.exp(f_last[k] - f0[k] + m_prev[k] - m_new[k]) for k in ks]
        kw = [(conv_out[g][2][h * HEAD_DIM:(h + 1) * HEAD_DIM, :] * jnp.exp(d_state[k] - m_new[k])).astype(BF16)
              for k, (h, g) in enumerate(ch)]
        for k, (h, g) in enumerate(ch):
            rows = slice((h % 2) * HEAD_DIM, (h % 2 + 1) * HEAD_DIM)
            cst_ref[g, h // 2, rows, :] = w_prev[k] * c_prev[k][rows, :] + _dot(kw[k], v_aug[k])
            mst_ref[g, h] = jnp.broadcast_to(m_new[k], mst_ref.shape[2:])

        ho = [hv[k] * _sigmoid(og_ref[g, :, v_lanes(h)].astype(F32)) for k, (h, g) in enumerate(ch)]
        yn = [ho[k] * lax.rsqrt(jnp.mean(ho[k] * ho[k], axis=1, keepdims=True) + RMS_EPS) * hg_ref[:, v_lanes(h)]
              for k, (h, g) in enumerate(ch)]
        for k, (h, g) in enumerate(ch):
            z = z_ref[g, :, v_lanes(h)].astype(F32)
            y_ref[g, :, v_lanes(h)] = (yn[k] * (z * _sigmoid(z))).astype(BF16)

    fprev_ref[...] = cc_ref[:, L - 1:L, :]


def _mlstm(p3, conv_w, cs, cst, rawt, head_g, *, L=256, n_rows=ML_ROWS):
    b, s, _ = p3.shape
    n_rows = min(n_rows, b)
    w512 = ML_HEADS * ML_V_DIM
    tile = lambda c: pl.BlockSpec((n_rows, L, w512), lambda bi, i, c=c: (bi, i, c))
    rowt = pl.BlockSpec((n_rows, GATE_ROWS, L), lambda bi, i: (bi, 0, i))
    return pl.pallas_call(
        functools.partial(_mlstm_kernel, L=L),
        grid=(b // n_rows, s // L),
        in_specs=[tile(COL_C + 0), tile(COL_C + 1), tile(COL_C + 2), tile(COL_C + 3),
                  pl.BlockSpec((ML_CONV, w512), lambda bi, i: (0, 0)),
                  pl.BlockSpec((n_rows, L, LANES), lambda bi, i: (bi, i, 0)),
                  rowt, rowt,
                  pl.BlockSpec((1, w512), lambda bi, i: (0, 0))],
        out_specs=pl.BlockSpec((n_rows, L, w512), lambda bi, i: (bi, i, 0)),
        out_shape=jax.ShapeDtypeStruct((b, s, w512), BF16),
        scratch_shapes=[pltpu.VMEM((n_rows, L + SUBLANES, w512), F32),
                        pltpu.VMEM((n_rows, ML_HEADS // 2, 2 * HEAD_DIM, 2 * ML_V_DIM), F32),
                        pltpu.VMEM((n_rows, ML_HEADS, SUBLANES, LANES), F32),
                        pltpu.VMEM((n_rows, 1, LANES), F32)],
        compiler_params=_compiler_params(2),
        name="mlstm",
    )(p3, p3, p3, p3, conv_w, cs, cst, rawt, head_g)


def _merge_kernel(ya_ref, yb_ref, yc_ref, g_ref, x_ref, wb_ref, wo_ref, fg_ref, o_ref, *, final):
    tm, d = x_ref.shape
    halves = [slice(r0, r0 + tm // 2) for r0 in (0, tm // 2)]
    y_refs = (ya_ref, yb_ref, yc_ref)
    branch = [[_dot(y_ref[rows, :], wb_ref[n]) for n, y_ref in enumerate(y_refs)] for rows in halves]
    for rows, outs in zip(halves, branch):
        merged = None
        for n, proj in enumerate(outs):
            term = _sigmoid(g_ref[rows, n * d:(n + 1) * d].astype(F32)) * proj
            merged = term if merged is None else merged + term
        out = x_ref[rows, :] + _dot(merged.astype(BF16), wo_ref[...])
        if final:
            out = out * lax.rsqrt(jnp.mean(out * out, axis=-1, keepdims=True) + RMS_EPS) * fg_ref[...]
        o_ref[rows, :] = out


def _merge(ya, yb, yc, p2, x2, wb, wo, fg, *, final, tm=512):
    t, d = x2.shape
    w = ya.shape[1]
    ytile = pl.BlockSpec((tm, w), lambda i: (i, 0))
    return pl.pallas_call(
        functools.partial(_merge_kernel, final=final),
        grid=(t // tm,),
        in_specs=[ytile, ytile, ytile,
                  pl.BlockSpec((tm, N_BRANCHES * d), lambda i: (i, GATE_COL0 // (N_BRANCHES * d))),
                  pl.BlockSpec((tm, d), lambda i: (i, 0)),
                  pl.BlockSpec((N_BRANCHES, w, d), lambda i: (0, 0, 0)),
                  pl.BlockSpec((d, d), lambda i: (0, 0)),
                  pl.BlockSpec((1, d), lambda i: (0, 0))],
        out_specs=pl.BlockSpec((tm, d), lambda i: (i, 0)),
        out_shape=jax.ShapeDtypeStruct((t, d), F32),
        compiler_params=_compiler_params(1),
        name="merge_final" if final else "merge",
    )(ya, yb, yc, p2, x2, wb, wo, fg)


def _gate_rows_and_scale(wt, layer):
    small = jnp.concatenate([wt[layer, FOX_GATE0:FOX_GATE0 + ATT_HEADS],
                             wt[layer, ML_GATE0:ML_GATE0 + 2 * ML_HEADS]], axis=0)
    small = jnp.pad(small, ((0, LANES - small.shape[0]), (0, 0)))
    col = jnp.arange(MAIN_WIDTH)
    is_q = (col < ATT_WIDTH) | ((col >= ATT_COLS) & (col < ATT_COLS + ATT_WIDTH))
    qscale = ATT_SCALE * LOG2E
    return small, jnp.where(is_q, qscale, 1.0).astype(F32).reshape(1, MAIN_WIDTH)


def kernel(x, norm_g, w_in, fox_b_f, mlstm_conv_w, mlstm_b_i, mlstm_b_f, mlstm_head_g, w_branch, w_out,
           final_norm_g):
    b, s, d = x.shape
    depth = w_in.shape[0]
    x2 = x.reshape(b * s, d)
    fg = final_norm_g.reshape(1, d)
    wt = jnp.swapaxes(w_in, 1, 2)
    for layer in range(depth):
        small, scale = _gate_rows_and_scale(wt, layer)
        bias = jnp.concatenate([fox_b_f[layer], mlstm_b_i[layer], mlstm_b_f[layer]])
        bias = jnp.pad(bias, (0, LANES - bias.shape[0])).reshape(1, LANES)

        p2, gs = _inproj(x2, norm_g[layer].reshape(1, d), wt, layer, scale, small)
        p3 = p2.reshape(b, s, MAIN_WIDTH)
        cs, cst, rawt = _gates(gs.reshape(b, s, LANES), bias)
        ya = _fox(p3, cs, cst)
        yb = _moba(p3)
        yc = _mlstm(p3, mlstm_conv_w[layer], cs, cst, rawt, mlstm_head_g[layer].reshape(1, -1))
        w512 = ya.shape[-1]
        x2 = _merge(ya.reshape(b * s, w512), yb.reshape(b * s, w512), yc.reshape(b * s, w512), p2, x2,
                    w_branch[layer].astype(BF16), w_out[layer].astype(BF16), fg,
                    final=(layer == depth - 1))
    return x2.reshape(b, s, d)
```

```python
import functools

import numpy as np
import jax
import jax.numpy as jnp
from jax import lax
from jax.experimental import pallas as pl
from jax.experimental.pallas import tpu as pltpu

F32 = jnp.float32
BF16 = jnp.bfloat16

D_MODEL = 1024
HEAD_DIM = 64
ATT_HEADS = 8
ATT_WIDTH = ATT_HEADS * HEAD_DIM
MOBA_BLOCK = 256
MOBA_TOPK = 3
ML_HEADS = 4
ML_V_DIM = 128
ML_CONV = 4
ML_ROWS = 2
ML_CHAINS = 4
N_BRANCHES = 3
RMS_EPS = 1e-6
NEG = -1e30
ATT_SCALE = HEAD_DIM ** -0.5
LOG2E = float(np.log2(np.e))
V_ROWS = 80

LANES = 128
SUBLANES = 8
VMEM_LIMIT_BYTES = 52 * 1024 * 1024

COL_A = 0
COL_B = 4
COL_C = 8
GATE_COL0 = 6144
MAIN_WIDTH = GATE_COL0 + N_BRANCHES * D_MODEL
LANE_FOX_F = 0
LANE_ML_I = 8
LANE_ML_F = 12
GATE_ROWS = 16
ATT_COLS = 4 * ATT_WIDTH
FOX_GATE0 = ATT_COLS
ML_GATE0 = FOX_GATE0 + ATT_HEADS + ATT_COLS + 4 * ML_HEADS * ML_V_DIM

ALIBI_SLOPES = tuple(
    float(v) for v in 2.0 ** (-8.0 * (np.arange(ATT_HEADS, dtype=np.float32) + 1.0) / ATT_HEADS))


def _dot(a, b):
    return jnp.dot(a, b, preferred_element_type=F32)


def _dot_nt(a, b):
    return lax.dot_general(a, b, (((1,), (1,)), ((), ())), preferred_element_type=F32)


def _sigmoid(x):
    return 1.0 / (1.0 + jnp.exp(-x))


def _compiler_params(n_axes):
    return pltpu.CompilerParams(dimension_semantics=("arbitrary",) * n_axes,
                                vmem_limit_bytes=VMEM_LIMIT_BYTES)


def _weight_row0(j, tn):
    skipped = (jnp.where(j * tn >= FOX_GATE0, ATT_HEADS, 0)
               + jnp.where(j * tn >= ML_GATE0 - ATT_HEADS, 2 * ML_HEADS, 0))
    return pl.multiple_of(j * tn + skipped, SUBLANES)


def _inproj_kernel(x_ref, g_ref, wt_hbm, sc_ref, ws_ref, bias_ref, p_ref, cs_ref, cst_ref, rawt_ref,
                   hn_ref, wbuf_ref, sem, *, layer, rc):
    i, j = pl.program_id(0), pl.program_id(1)
    n_i, n_j = pl.num_programs(0), pl.num_programs(1)
    tm, tn = p_ref.shape
    step = i * n_j + j
    slot = step & 1

    def weight_copy(tile, slot):
        return pltpu.make_async_copy(wt_hbm.at[layer, pl.ds(_weight_row0(tile, tn), tn), :],
                                     wbuf_ref.at[slot], sem.at[slot])

    @pl.when(step == 0)
    def _():
        weight_copy(j, slot).start()

    weight_copy(j, slot).wait()

    @pl.when(step + 1 < n_i * n_j)
    def _():
        weight_copy(jnp.where(j + 1 < n_j, j + 1, 0), 1 - slot).start()

    def weights():
        return wbuf_ref[slot].astype(BF16)

    def emit(rows, hn, w):
        p_ref[rows, :] = (_dot_nt(hn, w) * sc_ref[...]).astype(BF16)

    @pl.when(j == 0)
    def _():
        ws = ws_ref[...]
        wsh = ws.astype(BF16)
        ws2 = jnp.concatenate([wsh, (ws - wsh.astype(F32)).astype(BF16)], axis=0)
        w = weights()
        lane = lax.broadcasted_iota(jnp.int32, (1, LANES), 1)
        is_log_i = (lane >= LANE_ML_I) & (lane < LANE_ML_F)
        r = lax.broadcasted_iota(jnp.int32, (rc, rc), 0)
        c = lax.broadcasted_iota(jnp.int32, (rc, rc), 1)
        tri = jnp.where(r >= c, 1.0, 0.0).astype(BF16)
        carry = jnp.zeros((1, LANES), F32)
        for r0 in range(0, tm, rc):
            rows = slice(r0, r0 + rc)
            x = x_ref[rows, :]
            y = x * lax.rsqrt(jnp.mean(x * x, axis=-1, keepdims=True) + RMS_EPS) * g_ref[...]
            hi = y.astype(BF16)
            hn_ref[rows, :] = hi
            gs2 = _dot_nt(hi, ws2)
            emit(rows, hi, w)
            g = gs2[:, :LANES] + gs2[:, LANES:] + bias_ref[...]
            log_sig = jnp.minimum(g, 0.0) - jnp.log(1.0 + jnp.exp(-jnp.abs(g)))
            raw = jnp.where(is_log_i, g, log_sig)
            p_hi = raw.astype(BF16)
            r1 = raw - p_hi.astype(F32)
            p_mid = r1.astype(BF16)
            p_lo = (r1 - p_mid.astype(F32)).astype(BF16)
            cs = _dot(tri, p_hi) + _dot(tri, p_mid) + _dot(tri, p_lo) + carry
            carry = cs[rc - 1:rc, :]
            cs_ref[rows, :] = cs
            cst_ref[:, rows] = cs.T[:GATE_ROWS, :]
            rawt_ref[:, rows] = raw.T[:GATE_ROWS, :]

    @pl.when(j > 0)
    def _():
        emit(slice(None), hn_ref[...], weights())


def _inproj(x2, g, wt, layer, scale, ws, bias, seq, *, tn=1024, rc=256):
    t, d = x2.shape
    tm = seq
    b = t // seq
    n = scale.shape[1]
    return pl.pallas_call(
        functools.partial(_inproj_kernel, layer=layer, rc=rc),
        grid=(t // tm, n // tn),
        in_specs=[pl.BlockSpec((tm, d), lambda i, j: (i, 0)),
                  pl.BlockSpec((1, d), lambda i, j: (0, 0)),
                  pl.BlockSpec(memory_space=pl.ANY),
                  pl.BlockSpec((1, tn), lambda i, j: (0, j)),
                  pl.BlockSpec((LANES, d), lambda i, j: (0, 0)),
                  pl.BlockSpec((1, LANES), lambda i, j: (0, 0))],
        out_specs=[pl.BlockSpec((tm, tn), lambda i, j: (i, j)),
                   pl.BlockSpec((None, tm, LANES), lambda i, j: (i, 0, 0)),
                   pl.BlockSpec((None, GATE_ROWS, tm), lambda i, j: (i, 0, 0)),
                   pl.BlockSpec((None, GATE_ROWS, tm), lambda i, j: (i, 0, 0))],
        out_shape=[jax.ShapeDtypeStruct((t, n), BF16),
                   jax.ShapeDtypeStruct((b, seq, LANES), F32),
                   jax.ShapeDtypeStruct((b, GATE_ROWS, seq), F32),
                   jax.ShapeDtypeStruct((b, GATE_ROWS, seq), F32)],
        scratch_shapes=[pltpu.VMEM((tm, d), BF16),
                        pltpu.VMEM((2, tn, d), F32),
                        pltpu.SemaphoreType.DMA((2,))],
        compiler_params=_compiler_params(2),
        name="inproj",
    )(x2, g, wt, scale, ws, bias)


def _split3(x):
    hi = x.astype(BF16).astype(F32)
    r = x - hi
    mid = r.astype(BF16).astype(F32)
    lo = (r - mid).astype(BF16).astype(F32)
    return hi, mid, lo


def _split3_host(x):
    x = np.asarray(x, np.float32)
    hi = x.astype(BF16).astype(np.float32)
    mid = (x - hi).astype(BF16).astype(np.float32)
    lo = (x - hi - mid).astype(BF16).astype(np.float32)
    return hi, mid, lo


def _head_lane_mask(hh):
    lane = lax.broadcasted_iota(jnp.int32, (1, LANES), 1)
    return (lane >= HEAD_DIM * hh) & (lane < HEAD_DIM * (hh + 1))


def _store_qt(qt_ref, h, q_pair_t, aug_rows):
    chan = lax.broadcasted_iota(jnp.int32, (LANES, 1), 0)
    hh = h % 2
    own = (chan >= HEAD_DIM * hh) & (chan < HEAD_DIM * (hh + 1))
    pad = jnp.zeros((LANES - aug_rows.shape[0], aug_rows.shape[1]), F32)
    qt_ref[h] = jnp.concatenate([jnp.where(own, q_pair_t, 0.0), aug_rows, pad], axis=0).astype(BF16)


def _transpose_values(v_ref, vt_ref, blk):
    s = v_ref.shape[0]
    for b0 in range(0, s, blk):
        for hp in range(ATT_HEADS // 2):
            vb = v_ref[b0:b0 + blk, hp * LANES:(hp + 1) * LANES].astype(F32).T.astype(BF16)
            for hh in range(2):
                vt_ref[2 * hp + hh, 0:HEAD_DIM, b0:b0 + blk] = vb[hh * HEAD_DIM:(hh + 1) * HEAD_DIM, :]
    extra = V_ROWS - HEAD_DIM
    ones_row = jnp.where(lax.broadcasted_iota(jnp.int32, (extra, s), 0) == 0, 1.0, 0.0).astype(BF16)
    for h in range(ATT_HEADS):
        vt_ref[h, HEAD_DIM:V_ROWS, :] = ones_row


def _flash_all_heads(i, tq, k_ref, aug_ref, z_ref, o_ref, qt_ref, vt_ref):
    key = lax.broadcasted_iota(jnp.int32, (tq, tq), 0)
    qry = lax.broadcasted_iota(jnp.int32, (tq, tq), 1)
    causal = key <= qry

    def scores_of(start, diag):
        aug = aug_ref[pl.ds(start, tq), :]
        out = []
        for hp in range(ATT_HEADS // 2):
            kj = jnp.concatenate([k_ref[pl.ds(start, tq), hp * LANES:(hp + 1) * LANES], aug], axis=1)
            for h in (2 * hp, 2 * hp + 1):
                s = _dot(kj, qt_ref[h])
                out.append(jnp.where(causal, s, NEG) if diag else s)
        return out

    def absorb(start, scores, state):
        new_state = []
        for h in range(ATT_HEADS):
            vt = vt_ref[h, :, pl.ds(start, tq)]
            s = scores[h]
            if state is None:
                m = jnp.max(s, axis=0, keepdims=True)
                acc = _dot(vt, jnp.exp2(s - m).astype(BF16))
            else:
                m_old, acc_old = state[h]
                m = jnp.maximum(m_old, jnp.max(s, axis=0, keepdims=True))
                acc = jnp.exp2(m_old - m) * acc_old + _dot(vt, jnp.exp2(s - m).astype(BF16))
            new_state.append((m, acc))
        return tuple(new_state)

    def pair(jj, state):
        a = pl.multiple_of(2 * jj * tq, 2 * tq)
        b = pl.multiple_of(a + tq, tq)
        scores_a, scores_b = scores_of(a, False), scores_of(b, False)
        return absorb(b, scores_b, absorb(a, scores_a, state))

    def last_odd(_, state):
        start = pl.multiple_of((i - 1) * tq, tq)
        return absorb(start, scores_of(start, False), state)

    diag0 = pl.multiple_of(i * tq, tq)
    state = absorb(diag0, scores_of(diag0, True), None)
    state = lax.fori_loop(0, lax.shift_right_logical(i, 1), pair, state)
    state = lax.fori_loop(0, i & 1, last_odd, state)

    for hp in range(ATT_HEADS // 2):
        lanes = slice(hp * LANES, (hp + 1) * LANES)
        outs = [acc[:HEAD_DIM, :] / acc[HEAD_DIM:HEAD_DIM + 1, :] for _, acc in state[2 * hp:2 * hp + 2]]
        pair = jnp.concatenate(outs, axis=0).T
        z = z_ref[:, lanes].astype(F32)
        o_ref[:, lanes] = (pair * (z * _sigmoid(z))).astype(BF16)


def _flash_scratch(tq, s):
    return [pltpu.VMEM((ATT_HEADS, 2 * LANES, tq), BF16),
            pltpu.VMEM((ATT_HEADS, V_ROWS, s), BF16)]


FOX_ONES0 = ATT_HEADS * SUBLANES


def _fox_key_placement():
    pm = np.zeros((3 * LANES, LANES), np.float32)
    for h in range(ATT_HEADS):
        for p in range(3):
            pm[p * LANES + LANE_FOX_F + h, SUBLANES * h + p] = -1.0
    return jnp.asarray(pm, BF16)


def _fox_kernel(q_ref, k_ref, v_ref, z_ref, cc_ref, cr_ref, pm_ref, o_ref, aug_ref, qt_ref, vt_ref, *, tq):
    i = pl.program_id(1)
    lane = lax.broadcasted_iota(jnp.int32, (1, LANES), 1)

    @pl.when(i == 0)
    def _():
        _transpose_values(v_ref, vt_ref, tq)
        for b0 in range(0, k_ref.shape[0], tq):
            pieces = _split3(cc_ref[b0:b0 + tq, :] * LOG2E)
            aug = _dot(jnp.concatenate(pieces, axis=1).astype(BF16), pm_ref[...])
            aug_ref[b0:b0 + tq, :] = jnp.where((lane >= FOX_ONES0) & (lane < FOX_ONES0 + 3), 1.0, aug).astype(BF16)

    q0 = pl.multiple_of(i * tq, tq)
    c_hi, c_mid, c_lo = _split3(cr_ref[:, pl.ds(q0, tq)] * LOG2E)
    row = lax.broadcasted_iota(jnp.int32, (FOX_ONES0, tq), 0)
    for hp in range(ATT_HEADS // 2):
        qp = q_ref[:, hp * LANES:(hp + 1) * LANES].astype(F32).T
        for h in (2 * hp, 2 * hp + 1):
            r = LANE_FOX_F + h
            ones_rows = jnp.where((row >= SUBLANES * h) & (row < SUBLANES * h + 3), 1.0, 0.0)
            aug_rows = jnp.concatenate([ones_rows, c_hi[r:r + 1], c_mid[r:r + 1], c_lo[r:r + 1]], axis=0)
            _store_qt(qt_ref, h, qp, aug_rows)

    _flash_all_heads(i, tq, k_ref, aug_ref, z_ref, o_ref, qt_ref, vt_ref)


def _fox(p3, cs, cst, *, tq=256):
    b, s, _ = p3.shape
    tile = lambda c: pl.BlockSpec((None, tq, ATT_WIDTH), lambda bi, i, c=c: (bi, i, c))
    full = lambda c: pl.BlockSpec((None, s, ATT_WIDTH), lambda bi, i, c=c: (bi, 0, c))
    pm = _fox_key_placement()
    return pl.pallas_call(
        functools.partial(_fox_kernel, tq=tq),
        grid=(b, s // tq),
        in_specs=[tile(COL_A + 0), full(COL_A + 1), full(COL_A + 2), tile(COL_A + 3),
                  pl.BlockSpec((None, s, LANES), lambda bi, i: (bi, 0, 0)),
                  pl.BlockSpec((None, GATE_ROWS, s), lambda bi, i: (bi, 0, 0)),
                  pl.BlockSpec(pm.shape, lambda bi, i: (0, 0))],
        out_specs=pl.BlockSpec((None, tq, ATT_WIDTH), lambda bi, i: (bi, i, 0)),
        out_shape=jax.ShapeDtypeStruct((b, s, ATT_WIDTH), BF16),
        scratch_shapes=[pltpu.VMEM((s, LANES), BF16)] + _flash_scratch(tq, s),
        compiler_params=_compiler_params(2),
        name="fox",
    )(p3, p3, p3, p3, cs, cst, pm)


def _moba_constants(s, bs):
    slopes2 = np.asarray(ALIBI_SLOPES, np.float32) * np.float32(LOG2E)
    pos = np.arange(s)
    r_s = (pos % bs).astype(np.float32)[:, None]
    block_onehot = (pos[:, None] // bs == np.arange(SUBLANES)[None, :]).astype(np.float32)
    augk = np.concatenate([r_s, r_s, r_s, np.ones((s, 3), np.float32), np.zeros((s, 2), np.float32),
                           block_onehot, block_onehot, block_onehot,
                           np.zeros((s, LANES - 4 * SUBLANES), np.float32)], axis=1)
    r_t = np.arange(bs, dtype=np.float32)
    augq = []
    for h in range(ATT_HEADS):
        m_pieces = [np.full((bs,), p, np.float32) for p in _split3_host(slopes2[h])]
        augq.append(np.stack(m_pieces + list(_split3_host(-slopes2[h] * r_t))
                             + [np.zeros((bs,), np.float32)] * 2, axis=0))
    return jnp.asarray(augk, BF16), jnp.asarray(np.stack(augq), F32)


def _moba_kernel(q_ref, k_ref, v_ref, z_ref, augk_ref, augq_ref, o_ref, kmt_ref, qt_ref, vt_ref):
    i = pl.program_id(1)
    bs = MOBA_BLOCK
    nb = k_ref.shape[0] // bs

    @pl.when(i == 0)
    def _():
        _transpose_values(v_ref, vt_ref, bs)
        kms = [jnp.mean(k_ref[j * bs:(j + 1) * bs, :].astype(F32), axis=0, keepdims=True) for j in range(nb)]
        kms += [jnp.zeros((1, ATT_WIDTH), F32)] * (SUBLANES - nb)
        km_all = jnp.concatenate(kms, axis=0)
        head_of_lane = lax.broadcasted_iota(jnp.int32, (SUBLANES, ATT_WIDTH), 1) // HEAD_DIM
        for h in range(ATT_HEADS):
            kmt_ref[h * SUBLANES:(h + 1) * SUBLANES, :] = jnp.where(head_of_lane == h, km_all, 0.0)

    q_t = [q_ref[:, hp * LANES:(hp + 1) * LANES].astype(F32).T for hp in range(ATT_HEADS // 2)]
    gate_t = _dot(kmt_ref[...].astype(BF16), jnp.concatenate(q_t, axis=0).astype(BF16))
    blk = lax.broadcasted_iota(jnp.int32, (SUBLANES, bs), 0)
    valid = blk < i
    back = (i - blk).astype(F32)
    for h in range(ATT_HEADS):
        gate = jnp.where(valid, gate_t[h * SUBLANES:(h + 1) * SUBLANES, :], NEG)
        rank = jnp.zeros((SUBLANES, bs), F32)
        for r in range(1, SUBLANES):
            other = pltpu.roll(gate, r, axis=0)
            lower = blk >= r
            rank = rank + jnp.where(other > gate, 1.0, 0.0) + jnp.where((other == gate) & lower, 1.0, 0.0)
        keep = (valid & (rank < MOBA_TOPK)) | (blk == i)
        bias = jnp.where(keep, -(ALIBI_SLOPES[h] * LOG2E * bs) * back, NEG)
        aug_rows = jnp.concatenate([augq_ref[h]] + list(_split3(bias)), axis=0)
        _store_qt(qt_ref, h, q_t[h // 2], aug_rows)

    _flash_all_heads(i, bs, k_ref, augk_ref, z_ref, o_ref, qt_ref, vt_ref)


def _moba(p3):
    b, s, _ = p3.shape
    bs = MOBA_BLOCK
    assert s % bs == 0 and s // bs <= SUBLANES
    augk, augq = _moba_constants(s, bs)
    tile = lambda c: pl.BlockSpec((None, bs, ATT_WIDTH), lambda bi, i, c=c: (bi, i, c))
    full = lambda c: pl.BlockSpec((None, s, ATT_WIDTH), lambda bi, i, c=c: (bi, 0, c))
    return pl.pallas_call(
        _moba_kernel,
        grid=(b, s // bs),
        in_specs=[tile(COL_B + 0), full(COL_B + 1), full(COL_B + 2), tile(COL_B + 3),
                  pl.BlockSpec(augk.shape, lambda bi, i: (0, 0)),
                  pl.BlockSpec(augq.shape, lambda bi, i: (0, 0, 0))],
        out_specs=pl.BlockSpec((None, bs, ATT_WIDTH), lambda bi, i: (bi, i, 0)),
        out_shape=jax.ShapeDtypeStruct((b, s, ATT_WIDTH), BF16),
        scratch_shapes=[pltpu.VMEM((ATT_HEADS * SUBLANES, ATT_WIDTH), F32)] + _flash_scratch(bs, s),
        compiler_params=_compiler_params(2),
        name="moba",
    )(p3, p3, p3, p3, augk, augq)


def _mlstm_kernel(qk_ref, v_ref, og_ref, z_ref, w_ref, cc_ref, cr_ref, rr_ref, hg_ref, y_ref,
                  cbuf_ref, cst_ref, mst_ref, fprev_ref, *, L):
    c = pl.program_id(1)
    halo = SUBLANES
    n_rows = qk_ref.shape[0]

    @pl.when(c == 0)
    def _():
        cbuf_ref[:, 0:halo, :] = jnp.zeros((n_rows, halo, cbuf_ref.shape[2]), F32)
        cst_ref[...] = jnp.zeros_like(cst_ref)
        mst_ref[...] = jnp.zeros_like(mst_ref)
        fprev_ref[...] = jnp.zeros_like(fprev_ref)

    @pl.when(c > 0)
    def _():
        cbuf_ref[:, 0:halo, :] = cbuf_ref[:, L:L + halo, :]

    row = lax.broadcasted_iota(jnp.int32, (L, L), 0)
    col = lax.broadcasted_iota(jnp.int32, (L, L), 1)
    tri = row >= col
    lane1 = lax.broadcasted_iota(jnp.int32, (1, LANES), 1)
    ones_col = jnp.broadcast_to(jnp.where(lane1 == 0, 1.0, 0.0), (L, LANES)).astype(BF16)

    qk_w = ML_HEADS * HEAD_DIM

    conv_out = []
    for g in range(n_rows):
        cbuf_ref[g, halo:halo + L, :] = qk_ref[g].astype(F32)
        conv = None
        for j in range(ML_CONV):
            off = halo - (ML_CONV - 1) + j
            term = cbuf_ref[g, off:off + L, :] * w_ref[j:j + 1, :]
            conv = term if conv is None else conv + term
        qk = conv * _sigmoid(conv)
        k_s = qk[:, qk_w:] * ATT_SCALE
        conv_out.append((qk[:, :qk_w].astype(BF16), k_s.astype(BF16), k_s.T))

    def lane_pair(h):
        return slice((h // 2) * LANES, (h // 2 + 1) * LANES)

    def v_lanes(h):
        return slice(h * ML_V_DIM, (h + 1) * ML_V_DIM)

    all_chains = [(h, g) for h in range(ML_HEADS) for g in range(n_rows)]
    for c0 in range(0, len(all_chains), ML_CHAINS):
        ch = all_chains[c0:c0 + ML_CHAINS]
        ks = range(len(ch))
        qh = [jnp.where(_head_lane_mask(h % 2), conv_out[g][0][:, lane_pair(h)], jnp.zeros((L, LANES), BF16))
              for h, g in ch]
        f_c = [cc_ref[g, :, LANE_ML_F + h:LANE_ML_F + h + 1] for h, g in ch]
        g_r = [cr_ref[g, LANE_ML_F + h:LANE_ML_F + h + 1, :]
               - rr_ref[g, LANE_ML_I + h:LANE_ML_I + h + 1, :] for h, g in ch]
        f0 = [fprev_ref[g, 0:1, LANE_ML_F + h:LANE_ML_F + h + 1] for h, g in ch]
        m_prev = [mst_ref[g, h, 0:1, 0:1] for h, g in ch]
        qk_h = [_dot_nt(qh[k], conv_out[g][1][:, lane_pair(h)]) for k, (h, g) in enumerate(ch)]
        c_prev = [cst_ref[g, h // 2] for h, g in ch]
        inter = [_dot(qh[k], c_prev[k].astype(BF16)) for k in ks]
        d_intra = [jnp.where(tri, f_c[k] - g_r[k], NEG) for k in ks]
        d_inter = [f_c[k] - f0[k] + m_prev[k] for k in ks]
        m_t = [jnp.maximum(d_inter[k], jnp.max(d_intra[k], axis=1, keepdims=True)) for k in ks]
        w_intra = [jnp.exp(d_intra[k] - m_t[k]) for k in ks]
        w_inter = [jnp.exp(d_inter[k] - m_t[k]) for k in ks]
        s_mat = [(qk_h[k] * w_intra[k]).astype(BF16) for k in ks]
        v_aug = [jnp.concatenate([v_ref[g, :, v_lanes(h)], ones_col], axis=1) for h, g in ch]
        num = [_dot(s_mat[k], v_aug[k]) + w_inter[k] * inter[k] for k in ks]
        hv = [num[k][:, :ML_V_DIM] / jnp.maximum(jnp.abs(num[k][:, ML_V_DIM:ML_V_DIM + 1]), jnp.exp(-m_t[k]))
              for k in ks]

        f_last = [f_c[k][L - 1:L, :] for k in ks]
        d_state = [f_last[k] - g_r[k] for k in ks]
        m_new = [jnp.maximum(f_last[k] - f0[k] + m_prev[k], jnp.max(d_state[k], axis=1, keepdims=True))
                 for k in ks]
        w_prev = [jnp.exp(f_last[k] - f0[k] + m_prev[k] - m_new[k]) for k in ks]
        kw = [(conv_out[g][2][h * HEAD_DIM:(h + 1) * HEAD_DIM, :] * jnp.exp(d_state[k] - m_new[k])).astype(BF16)
              for k, (h, g) in enumerate(ch)]
        for k, (h, g) in enumerate(ch):
            rows = slice((h % 2) * HEAD_DIM, (h % 2 + 1) * HEAD_DIM)
            cst_ref[g, h // 2, rows, :] = w_prev[k] * c_prev[k][rows, :] + _dot(kw[k], v_aug[k])
            mst_ref[g, h] = jnp.broadcast_to(m_new[k], mst_ref.shape[2:])

        ho = [hv[k] * _sigmoid(og_ref[g, :, v_lanes(h)].astype(F32)) for k, (h, g) in enumerate(ch)]
        yn = [ho[k] * lax.rsqrt(jnp.mean(ho[k] * ho[k], axis=1, keepdims=True) + RMS_EPS) * hg_ref[:, v_lanes(h)]
              for k, (h, g) in enumerate(ch)]
        for k, (h, g) in enumerate(ch):
            z = z_ref[g, :, v_lanes(h)].astype(F32)
            y_ref[g, :, v_lanes(h)] = (yn[k] * (z * _sigmoid(z))).astype(BF16)

    fprev_ref[...] = cc_ref[:, L - 1:L, :]


def _mlstm(p3, conv_w, cs, cst, rawt, head_g, *, L=256, n_rows=ML_ROWS):
    b, s, _ = p3.shape
    n_rows = min(n_rows, b)
    w512 = ML_HEADS * ML_V_DIM
    tile = lambda c: pl.BlockSpec((n_rows, L, w512), lambda bi, i, c=c: (bi, i, c))
    rowt = pl.BlockSpec((n_rows, GATE_ROWS, L), lambda bi, i: (bi, 0, i))
    return pl.pallas_call(
        functools.partial(_mlstm_kernel, L=L),
        grid=(b // n_rows, s // L),
        in_specs=[tile(COL_C + 0), tile(COL_C + 1), tile(COL_C + 2), tile(COL_C + 3),
                  pl.BlockSpec((ML_CONV, w512), lambda bi, i: (0, 0)),
                  pl.BlockSpec((n_rows, L, LANES), lambda bi, i: (bi, i, 0)),
                  rowt, rowt,
                  pl.BlockSpec((1, w512), lambda bi, i: (0, 0))],
        out_specs=pl.BlockSpec((n_rows, L, w512), lambda bi, i: (bi, i, 0)),
        out_shape=jax.ShapeDtypeStruct((b, s, w512), BF16),
        scratch_shapes=[pltpu.VMEM((n_rows, L + SUBLANES, w512), F32),
                        pltpu.VMEM((n_rows, ML_HEADS // 2, 2 * HEAD_DIM, 2 * ML_V_DIM), F32),
                        pltpu.VMEM((n_rows, ML_HEADS, SUBLANES, LANES), F32),
                        pltpu.VMEM((n_rows, 1, LANES), F32)],
        compiler_params=_compiler_params(2),
        name="mlstm",
    )(p3, p3, p3, p3, conv_w, cs, cst, rawt, head_g)


def _merge_kernel(ya_ref, yb_ref, yc_ref, g_ref, x_ref, wb_ref, wo_ref, fg_ref, o_ref, *, final):
    d = x_ref.shape[1]
    merged = None
    for n, y_ref in enumerate((ya_ref, yb_ref, yc_ref)):
        gate = _sigmoid(g_ref[:, n * d:(n + 1) * d].astype(F32))
        term = gate * _dot(y_ref[...], wb_ref[n])
        merged = term if merged is None else merged + term
    out = x_ref[...] + _dot(merged.astype(BF16), wo_ref[...])
    if final:
        out = out * lax.rsqrt(jnp.mean(out * out, axis=-1, keepdims=True) + RMS_EPS) * fg_ref[...]
    o_ref[...] = out


def _merge(ya, yb, yc, p2, x2, wb, wo, fg, *, final, tm=512):
    t, d = x2.shape
    w = ya.shape[1]
    ytile = pl.BlockSpec((tm, w), lambda i: (i, 0))
    return pl.pallas_call(
        functools.partial(_merge_kernel, final=final),
        grid=(t // tm,),
        in_specs=[ytile, ytile, ytile,
                  pl.BlockSpec((tm, N_BRANCHES * d), lambda i: (i, GATE_COL0 // (N_BRANCHES * d))),
                  pl.BlockSpec((tm, d), lambda i: (i, 0)),
                  pl.BlockSpec((N_BRANCHES, w, d), lambda i: (0, 0, 0)),
                  pl.BlockSpec((d, d), lambda i: (0, 0)),
                  pl.BlockSpec((1, d), lambda i: (0, 0))],
        out_specs=pl.BlockSpec((tm, d), lambda i: (i, 0)),
        out_shape=jax.ShapeDtypeStruct((t, d), F32),
        compiler_params=_compiler_params(1),
        name="merge_final" if final else "merge",
    )(ya, yb, yc, p2, x2, wb, wo, fg)


def _gate_rows_and_scale(wt, layer):
    small = jnp.concatenate([wt[layer, FOX_GATE0:FOX_GATE0 + ATT_HEADS],
                             wt[layer, ML_GATE0:ML_GATE0 + 2 * ML_HEADS]], axis=0)
    small = jnp.pad(small, ((0, LANES - small.shape[0]), (0, 0)))
    col = jnp.arange(MAIN_WIDTH)
    is_q = (col < ATT_WIDTH) | ((col >= ATT_COLS) & (col < ATT_COLS + ATT_WIDTH))
    qscale = ATT_SCALE * LOG2E
    return small, jnp.where(is_q, qscale, 1.0).astype(F32).reshape(1, MAIN_WIDTH)


def kernel(x, norm_g, w_in, fox_b_f, mlstm_conv_w, mlstm_b_i, mlstm_b_f, mlstm_head_g, w_branch, w_out,
           final_norm_g):
    b, s, d = x.shape
    depth = w_in.shape[0]
    x2 = x.reshape(b * s, d)
    fg = final_norm_g.reshape(1, d)
    wt = jnp.swapaxes(w_in, 1, 2)
    for layer in range(depth):
        small, scale = _gate_rows_and_scale(wt, layer)
        bias = jnp.concatenate([fox_b_f[layer], mlstm_b_i[layer], mlstm_b_f[layer]])
        bias = jnp.pad(bias, (0, LANES - bias.shape[0])).reshape(1, LANES)

        p2, cs, cst, rawt = _inproj(x2, norm_g[layer].reshape(1, d), wt, layer, scale, small, bias, s)
        p3 = p2.reshape(b, s, MAIN_WIDTH)
        ya = _fox(p3, cs, cst)
        yb = _moba(p3)
        yc = _mlstm(p3, mlstm_conv_w[layer], cs, cst, rawt, mlstm_head_g[layer].reshape(1, -1))
        w512 = ya.shape[-1]
        x2 = _merge(ya.reshape(b * s, w512), yb.reshape(b * s, w512), yc.reshape(b * s, w512), p2, x2,
                    w_branch[layer].astype(BF16), w_out[layer].astype(BF16), fg,
                    final=(layer == depth - 1))
    return x2.reshape(b, s, d)
```
